```python
import math
import jax, jax.numpy as jnp
from jax import lax
import numpy as np

D_MODEL = 1024
BATCH = 16
SEQ = 256
DEPTH = 4
DEC_BATCH = 8
DEC_SEQ = 4096
PAST_LEN = 512

GRID_W = 64
HEAD_DIM = 64
N_BRANCH = 4
BRANCH_W = 256
NA_HEADS = 4
NA_KH = 8
NA_KW = 16
DIFF_HEADS = 4
DIFF_DH = 32
MLA_HEADS = 4
MLA_Q_LORA = 256
MLA_KV_LORA = 128
MLA_NOPE = 64
MLA_ROPE = 32
MLA_V = 64
GQA_HEADS = 4
GQA_KV_HEADS = 2
MOE_GROUPS = 4
MOE_EPG = 8
MOE_EXPERTS = 32
MOE_FF = 128
N_MOD = 6
ROPE_BASE = 10000.0
EPS = 1e-6
Q_BLOCK = 128
IN_SIZES = (NA_HEADS * HEAD_DIM, NA_HEADS * HEAD_DIM, NA_HEADS * HEAD_DIM,
            DIFF_HEADS * 2 * DIFF_DH, DIFF_HEADS * 2 * DIFF_DH, DIFF_HEADS * HEAD_DIM,
            MLA_Q_LORA, MLA_KV_LORA, MLA_ROPE,
            GQA_HEADS * HEAD_DIM, GQA_KV_HEADS * HEAD_DIM, GQA_KV_HEADS * HEAD_DIM,
            N_BRANCH * D_MODEL)
IN_TOTAL = sum(IN_SIZES)

kernel_name = "hybrid_diffusion_prefix_trunk_step"


def rms_norm(x, g):
    xf = x.astype(jnp.float32)
    y = xf * lax.rsqrt(jnp.mean(xf * xf, axis=-1, keepdims=True) + EPS)
    return y.astype(x.dtype) * g


def axial_angles(L, rot_dim):
    t = jnp.arange(L)
    row = (t // GRID_W).astype(jnp.float32)
    col = (t % GRID_W).astype(jnp.float32)
    half = rot_dim // 2
    freqs = ROPE_BASE ** (-jnp.arange(0, half, 2, dtype=jnp.float32) / half)
    return row[:, None] * freqs, col[:, None] * freqs


def rope_1d(x, ang):
    cos = jnp.cos(ang)[None, :, None, :]
    sin = jnp.sin(ang)[None, :, None, :]
    x1, x2 = jnp.split(x.astype(jnp.float32), 2, axis=-1)
    return jnp.concatenate([x1 * cos - x2 * sin, x2 * cos + x1 * sin], axis=-1).astype(x.dtype)


def axial_rope(x):
    ang_r, ang_c = axial_angles(x.shape[1], x.shape[-1])
    xr, xc = jnp.split(x, 2, axis=-1)
    return jnp.concatenate([rope_1d(xr, ang_r), rope_1d(xc, ang_c)], axis=-1)


def rope_tail(x):
    return jnp.concatenate([x[..., :MLA_NOPE], axial_rope(x[..., MLA_NOPE:])], axis=-1)


def map_query_blocks(f, q):
    B, L = q.shape[:2]
    nb = L // Q_BLOCK
    qb = jnp.moveaxis(q.reshape((B, nb, Q_BLOCK) + q.shape[2:]), 1, 0)
    out = jnp.moveaxis(lax.map(f, qb), 0, 1)
    return out.reshape((B, L) + out.shape[3:])


def gqa_attend(q, k, v):
    B, _, Hq, dq = q.shape
    Hk, dv = k.shape[2], v.shape[-1]
    rep = Hq // Hk
    scale = dq ** -0.5

    def blk(qb):
        qb = qb.reshape(B, Q_BLOCK, Hk, rep, dq)
        s = jnp.einsum("bqgrd,bkgd->bgrqk", qb, k).astype(jnp.float32) * scale
        p = jax.nn.softmax(s, axis=-1).astype(v.dtype)
        o = jnp.einsum("bgrqk,bkgd->bqgrd", p, v)
        return o.reshape(B, Q_BLOCK, Hq * dv)

    return map_query_blocks(blk, q)


def diff_attend(q, k, v, lam):
    scale = q.shape[-1] ** -0.5

    def blk(qb):
        s = jnp.einsum("bqhmd,bkhmd->bhmqk", qb, k).astype(jnp.float32) * scale
        p = jax.nn.softmax(s, axis=-1)
        a = p[:, :, 0] - lam * p[:, :, 1]
        return jnp.einsum("bhqk,bkhd->bqhd", a.astype(v.dtype), v)

    return map_query_blocks(blk, q)


def diff_branch(q, k, v, lp, lam_init):
    B, L = q.shape[:2]
    lam = (jnp.exp(jnp.sum((lp["diff_lq1"] * lp["diff_lk1"]).astype(jnp.float32)))
           - jnp.exp(jnp.sum((lp["diff_lq2"] * lp["diff_lk2"]).astype(jnp.float32))) + lam_init)
    o = diff_attend(q, k, v, lam)
    o = rms_norm(o, lp["diff_sub_g"]) * (1.0 - lam_init)
    return o.reshape(B, L, DIFF_HEADS * HEAD_DIM)


def neighborhood_attend(q, k, v, k_ctx, v_ctx, rpb):
    B, L, H, d = q.shape
    rows = L // GRID_W
    kh = min(NA_KH, rows)
    kw = NA_KW
    scale = d ** -0.5
    kg = k.reshape(B, rows, GRID_W, H, d)
    vg = v.reshape(B, rows, GRID_W, H, d)
    qg = jnp.moveaxis(q.reshape(B, rows, GRID_W, H, d), 1, 0)
    col = jnp.arange(GRID_W)
    cs = jnp.clip(col - kw // 2, 0, GRID_W - kw)
    col_ok = (col[None, :] >= cs[:, None]) & (col[None, :] < cs[:, None] + kw)
    dc_idx = jnp.clip(col[None, :] - col[:, None], -(kw - 1), kw - 1) + (NA_KW - 1)

    def row_block(args):
        q_row, r = args
        rs = jnp.clip(r - kh // 2, 0, rows - kh)
        k_band = lax.dynamic_slice_in_dim(kg, rs, kh, axis=1)
        v_band = lax.dynamic_slice_in_dim(vg, rs, kh, axis=1)
        dr_idx = rs + jnp.arange(kh) - r + (NA_KH - 1)
        bias = rpb[:, dr_idx][:, :, dc_idx]
        s_band = jnp.einsum("bqhd,bikhd->bhqik", q_row, k_band).astype(jnp.float32) * scale
        s_band = s_band + jnp.transpose(bias, (0, 2, 1, 3))[None].astype(jnp.float32)
        s_band = jnp.where(col_ok[None, None, :, None, :], s_band, -jnp.inf)
        s_ctx = jnp.einsum("bqhd,bkhd->bhqk", q_row, k_ctx).astype(jnp.float32) * scale
        s = jnp.concatenate([s_band.reshape(B, H, GRID_W, kh * GRID_W), s_ctx], axis=-1)
        p = jax.nn.softmax(s, axis=-1).astype(v.dtype)
        p_band = p[..., :kh * GRID_W].reshape(B, H, GRID_W, kh, GRID_W)
        return (jnp.einsum("bhqik,bikhd->bqhd", p_band, v_band)
                + jnp.einsum("bhqk,bkhd->bqhd", p[..., kh * GRID_W:], v_ctx))

    out = lax.map(row_block, (qg, jnp.arange(rows)))
    return jnp.moveaxis(out, 0, 1).reshape(B, L, H * d)


def mixer_inputs(h, lp):
    B, L, _ = h.shape
    z = h @ lp["w_in"]
    cuts = np.cumsum(np.array(IN_SIZES))[:-1].tolist()
    (na_q, na_k, na_v, df_q, df_k, df_v, cq, ckv, kr, gq, gk, gv, gates) = jnp.split(z, cuts, axis=-1)
    na_q = rms_norm(na_q.reshape(B, L, NA_HEADS, HEAD_DIM), lp["na_q_g"])
    na_k = rms_norm(na_k.reshape(B, L, NA_HEADS, HEAD_DIM), lp["na_k_g"])
    na_v = na_v.reshape(B, L, NA_HEADS, HEAD_DIM)
    df_q = rms_norm(df_q.reshape(B, L, DIFF_HEADS, 2, DIFF_DH), lp["diff_q_g"])
    df_k = rms_norm(df_k.reshape(B, L, DIFF_HEADS, 2, DIFF_DH), lp["diff_k_g"])
    df_v = df_v.reshape(B, L, DIFF_HEADS, HEAD_DIM)
    mla_q = (rms_norm(cq, lp["mla_q_a_g"]) @ lp["mla_w_q_b"]).reshape(B, L, MLA_HEADS, MLA_NOPE + MLA_ROPE)
    mla_q = rms_norm(mla_q, lp["mla_q_g"])
    ckv = rms_norm(ckv, lp["mla_kv_a_g"])
    gq = rms_norm(gq.reshape(B, L, GQA_HEADS, HEAD_DIM), lp["gqa_q_g"])
    gk = rms_norm(gk.reshape(B, L, GQA_KV_HEADS, HEAD_DIM), lp["gqa_k_g"])
    gv = gv.reshape(B, L, GQA_KV_HEADS, HEAD_DIM)
    gates = jax.nn.sigmoid(gates.reshape(B, L, N_BRANCH, D_MODEL))
    return (na_q, na_k, na_v, df_q, df_k, df_v, mla_q, ckv, kr, gq, gk, gv, gates)


def mla_keys(ckv, kr, lp):
    B, L, _ = ckv.shape
    kv = (ckv @ lp["mla_w_kv_b"]).reshape(B, L, MLA_HEADS, MLA_NOPE + MLA_V)
    k_nope, v = kv[..., :MLA_NOPE], kv[..., MLA_NOPE:]
    k = jnp.concatenate([k_nope, jnp.broadcast_to(kr[:, :, None, :], (B, L, MLA_HEADS, MLA_ROPE))], axis=-1)
    return rms_norm(k, lp["mla_k_g"]), v


def merge_branches(outs, gates, lp):
    y = gates[:, :, 0] * (outs[0] @ lp["w_branch"][0])
    for m in range(1, N_BRANCH):
        y = y + gates[:, :, m] * (outs[m] @ lp["w_branch"][m])
    return y @ lp["w_out"]


def hier_moe(h, lp):
    B, L, _ = h.shape
    pg = jax.nn.softmax((h @ lp["moe_w_group"]).astype(jnp.float32), axis=-1)
    g_top, g_idx = lax.top_k(pg, 1)
    le = (h @ lp["moe_w_expert"]).astype(jnp.float32).reshape(B, L, MOE_GROUPS, MOE_EPG)
    le_sel = jnp.take_along_axis(le, g_idx[..., None], axis=2)[:, :, 0]
    pe = jax.nn.softmax(le_sel, axis=-1)
    e_top, e_idx = lax.top_k(pe, 2)
    e_top = e_top / jnp.sum(e_top, axis=-1, keepdims=True)
    expert_id = g_idx * MOE_EPG + e_idx
    gate = jnp.sum(jax.nn.one_hot(expert_id, MOE_EXPERTS, dtype=jnp.float32)
                   * (g_top * e_top)[..., None], axis=2)
    hid = jax.nn.silu(h @ lp["moe_w1"]) * (h @ lp["moe_w3"])
    hid = hid.reshape(B, L, MOE_EXPERTS, MOE_FF) * gate[..., None].astype(h.dtype)
    return jnp.einsum("blef,efd->bld", hid, lp["moe_w2"])


def modulation(cond, lp):
    m = jax.nn.silu(cond) @ lp["w_mod"] + lp["b_mod"]
    m = m.reshape(cond.shape[0], 1, N_MOD, D_MODEL)
    return [m[:, :, i] for i in range(N_MOD)]


def context_layer(x, cond, lp, lam_init):
    B, L, _ = x.shape
    sh1, sc1, ga1, sh2, sc2, ga2 = modulation(cond, lp)
    h = rms_norm(x, lp["norm1_g"]) * (1.0 + sc1) + sh1
    (na_q, na_k, na_v, df_q, df_k, df_v, mla_q, ckv, kr, gq, gk, gv, gates) = mixer_inputs(h, lp)
    o_na = gqa_attend(na_q, na_k, na_v)
    o_df = diff_branch(df_q, df_k, df_v, lp, lam_init)
    k_mla, v_mla = mla_keys(ckv, kr, lp)
    o_mla = gqa_attend(mla_q, k_mla, v_mla)
    o_gqa = gqa_attend(gq, gk, gv)
    x = x + ga1 * merge_branches([o_na, o_df, o_mla, o_gqa], gates, lp)
    h2 = rms_norm(x, lp["norm2_g"]) * (1.0 + sc2) + sh2
    x = x + ga2 * hier_moe(h2, lp)
    state = (na_k, na_v, df_k.reshape(B, L, DIFF_HEADS, 2 * DIFF_DH), df_v, ckv, kr, gk, gv)
    return x, state


def latent_layer(x, cond, cache, lp, lam_init):
    c_na_k, c_na_v, c_df_k, c_df_v, c_ckv, c_kr, c_gk, c_gv = cache
    Bc, Lc = c_na_k.shape[:2]
    sh1, sc1, ga1, sh2, sc2, ga2 = modulation(cond, lp)
    h = rms_norm(x, lp["norm1_g"]) * (1.0 + sc1) + sh1
    (na_q, na_k, na_v, df_q, df_k, df_v, mla_q, ckv, kr, gq, gk, gv, gates) = mixer_inputs(h, lp)
    B, L = x.shape[:2]
    o_na = neighborhood_attend(na_q, na_k, na_v, c_na_k, c_na_v, lp["na_rpb"])
    q_df = axial_rope(df_q.reshape(B, L, DIFF_HEADS * 2, DIFF_DH)).reshape(B, L, DIFF_HEADS, 2, DIFF_DH)
    k_df = axial_rope(df_k.reshape(B, L, DIFF_HEADS * 2, DIFF_DH)).reshape(B, L, DIFF_HEADS, 2, DIFF_DH)
    k_df = jnp.concatenate([k_df, c_df_k.reshape(Bc, Lc, DIFF_HEADS, 2, DIFF_DH)], axis=1)
    v_df = jnp.concatenate([df_v, c_df_v], axis=1)
    o_df = diff_branch(q_df, k_df, v_df, lp, lam_init)
    k_lat, v_lat = mla_keys(ckv, kr, lp)
    k_ctx, v_ctx = mla_keys(c_ckv, c_kr, lp)
    k_mla = jnp.concatenate([rope_tail(k_lat), k_ctx], axis=1)
    v_mla = jnp.concatenate([v_lat, v_ctx], axis=1)
    o_mla = gqa_attend(rope_tail(mla_q), k_mla, v_mla)
    k_g = jnp.concatenate([axial_rope(gk), c_gk], axis=1)
    v_g = jnp.concatenate([gv, c_gv], axis=1)
    o_gqa = gqa_attend(axial_rope(gq), k_g, v_g)
    x = x + ga1 * merge_branches([o_na, o_df, o_mla, o_gqa], gates, lp)
    h2 = rms_norm(x, lp["norm2_g"]) * (1.0 + sc2) + sh2
    return x + ga2 * hier_moe(h2, lp)


def setup_inputs(seed: int = 0) -> dict:
    key = jax.random.key(seed)
    ks = iter(jax.random.split(key, 64))
    D = D_MODEL

    def nrm(shape, scale=1.0):
        return scale * jax.random.normal(next(ks), shape, jnp.float32)

    def gain(shape):
        return 1.0 + 0.1 * nrm(shape)

    cb = (DEC_BATCH, DEPTH, PAST_LEN)
    return {
        "x_prompt": nrm((BATCH, SEQ, D)),
        "x_sample": nrm((DEC_BATCH, DEC_SEQ, D)),
        "cache_na_k": nrm(cb + (NA_HEADS, HEAD_DIM)),
        "cache_na_v": nrm(cb + (NA_HEADS, HEAD_DIM)),
        "cache_diff_k": nrm(cb + (DIFF_HEADS, 2 * DIFF_DH)),
        "cache_diff_v": nrm(cb + (DIFF_HEADS, HEAD_DIM)),
        "cache_mla_ckv": nrm(cb + (MLA_KV_LORA,)),
        "cache_mla_krope": nrm(cb + (MLA_ROPE,)),
        "cache_gqa_k": nrm(cb + (GQA_KV_HEADS, HEAD_DIM)),
        "cache_gqa_v": nrm(cb + (GQA_KV_HEADS, HEAD_DIM)),
        "c": nrm((DEC_BATCH, D)),
        "c_ctx": nrm((D,)),
        "w_mod": nrm((DEPTH, D, N_MOD * D), 0.5 * D ** -0.5),
        "b_mod": nrm((DEPTH, N_MOD * D), 0.01),
        "norm1_g": gain((DEPTH, D)),
        "norm2_g": gain((DEPTH, D)),
        "w_in": nrm((DEPTH, D, IN_TOTAL), D ** -0.5),
        "na_q_g": gain((DEPTH, HEAD_DIM)),
        "na_k_g": gain((DEPTH, HEAD_DIM)),
        "na_rpb": nrm((DEPTH, NA_HEADS, 2 * NA_KH - 1, 2 * NA_KW - 1), 0.1),
        "diff_q_g": gain((DEPTH, DIFF_DH)),
        "diff_k_g": gain((DEPTH, DIFF_DH)),
        "diff_lq1": nrm((DEPTH, DIFF_DH), 0.1),
        "diff_lk1": nrm((DEPTH, DIFF_DH), 0.1),
        "diff_lq2": nrm((DEPTH, DIFF_DH), 0.1),
        "diff_lk2": nrm((DEPTH, DIFF_DH), 0.1),
        "diff_sub_g": gain((DEPTH, HEAD_DIM)),
        "mla_q_a_g": gain((DEPTH, MLA_Q_LORA)),
        "mla_w_q_b": nrm((DEPTH, MLA_Q_LORA, MLA_HEADS * (MLA_NOPE + MLA_ROPE)), MLA_Q_LORA ** -0.5),
        "mla_kv_a_g": gain((DEPTH, MLA_KV_LORA)),
        "mla_w_kv_b": nrm((DEPTH, MLA_KV_LORA, MLA_HEADS * (MLA_NOPE + MLA_V)), MLA_KV_LORA ** -0.5),
        "mla_q_g": gain((DEPTH, MLA_NOPE + MLA_ROPE)),
        "mla_k_g": gain((DEPTH, MLA_NOPE + MLA_ROPE)),
        "gqa_q_g": gain((DEPTH, HEAD_DIM)),
        "gqa_k_g": gain((DEPTH, HEAD_DIM)),
        "w_branch": nrm((DEPTH, N_BRANCH, BRANCH_W, D), BRANCH_W ** -0.5),
        "w_out": nrm((DEPTH, D, D), D ** -0.5),
        "moe_w_group": nrm((DEPTH, D, MOE_GROUPS), D ** -0.5),
        "moe_w_expert": nrm((DEPTH, D, MOE_EXPERTS), D ** -0.5),
        "moe_w1": nrm((DEPTH, D, MOE_EXPERTS * MOE_FF), D ** -0.5),
        "moe_w3": nrm((DEPTH, D, MOE_EXPERTS * MOE_FF), D ** -0.5),
        "moe_w2": nrm((DEPTH, MOE_EXPERTS, MOE_FF, D), MOE_FF ** -0.5),
    }


def reference(x_prompt, x_sample, cache_na_k, cache_na_v, cache_diff_k, cache_diff_v,
              cache_mla_ckv, cache_mla_krope, cache_gqa_k, cache_gqa_v, c, c_ctx,
              w_mod, b_mod, norm1_g, norm2_g, w_in, na_q_g, na_k_g, na_rpb,
              diff_q_g, diff_k_g, diff_lq1, diff_lk1, diff_lq2, diff_lk2, diff_sub_g,
              mla_q_a_g, mla_w_q_b, mla_kv_a_g, mla_w_kv_b, mla_q_g, mla_k_g,
              gqa_q_g, gqa_k_g, w_branch, w_out, moe_w_group, moe_w_expert,
              moe_w1, moe_w3, moe_w2):
    xp = x_prompt
    xs = x_sample
    states = [[] for _ in range(8)]
    for l in range(DEPTH):
        lp = dict(w_mod=w_mod[l], b_mod=b_mod[l], norm1_g=norm1_g[l], norm2_g=norm2_g[l],
                  w_in=w_in[l], na_q_g=na_q_g[l], na_k_g=na_k_g[l], na_rpb=na_rpb[l],
                  diff_q_g=diff_q_g[l], diff_k_g=diff_k_g[l], diff_lq1=diff_lq1[l],
                  diff_lk1=diff_lk1[l], diff_lq2=diff_lq2[l], diff_lk2=diff_lk2[l],
                  diff_sub_g=diff_sub_g[l], mla_q_a_g=mla_q_a_g[l], mla_w_q_b=mla_w_q_b[l],
                  mla_kv_a_g=mla_kv_a_g[l], mla_w_kv_b=mla_w_kv_b[l], mla_q_g=mla_q_g[l],
                  mla_k_g=mla_k_g[l], gqa_q_g=gqa_q_g[l], gqa_k_g=gqa_k_g[l],
                  w_branch=w_branch[l], w_out=w_out[l], moe_w_group=moe_w_group[l],
                  moe_w_expert=moe_w_expert[l], moe_w1=moe_w1[l], moe_w3=moe_w3[l],
                  moe_w2=moe_w2[l])
        lam_init = 0.8 - 0.6 * math.exp(-0.3 * l)
        xp, ctx_state = context_layer(xp, c_ctx[None, :], lp, lam_init)
        for i in range(8):
            states[i].append(ctx_state[i])
        cache = (cache_na_k[:, l], cache_na_v[:, l], cache_diff_k[:, l], cache_diff_v[:, l],
                 cache_mla_ckv[:, l], cache_mla_krope[:, l], cache_gqa_k[:, l], cache_gqa_v[:, l])
        xs = latent_layer(xs, c, cache, lp, lam_init)
    new_na_k = jnp.stack(states[0], axis=1)
    new_na_v = jnp.stack(states[1], axis=1)
    new_diff_k = jnp.stack(states[2], axis=1)
    new_diff_v = jnp.stack(states[3], axis=1)
    new_mla_ckv = jnp.stack(states[4], axis=1)
    new_mla_krope = jnp.stack(states[5], axis=1)
    new_gqa_k = jnp.stack(states[6], axis=1)
    new_gqa_v = jnp.stack(states[7], axis=1)
    return (xp, xs, new_na_k, new_na_v, new_diff_k, new_diff_v, new_mla_ckv, new_mla_krope, new_gqa_k, new_gqa_v)
```

```python
import functools
import math

import numpy as np
import jax
import jax.numpy as jnp
from jax import lax
from jax.experimental import pallas as pl
from jax.experimental.pallas import tpu as pltpu

F32 = jnp.float32
BF16 = jnp.bfloat16

D_MODEL = 1024
GRID_W = 64
HEAD_DIM = 64
NA_KH = 8
NA_KW = 16
DIFF_DH = 32
MLA_HEAD_PAD = 128
MLA_QK = 96
MLA_NOPE = 64
MLA_ROPE = 32
MOE_GROUPS = 4
MOE_EPG = 8
MOE_EXPERTS = 32
MOE_FF = 128
N_MOD = 6
ROPE_BASE = 10000.0
EPS = 1e-6
NEG = -1e30
LANE = 128
EXPERT_LANE0 = 32
MOE_CHUNKS = 8
MOE_CHUNK_W = MOE_EXPERTS * MOE_FF // MOE_CHUNKS

P_NAQ, P_NAK, P_NAV = 0, 256, 512
P_DFQ, P_DFK, P_DFV = 768, 1024, 1280
P_MQ, P_MK, P_MV = 1536, 2048, 2560
P_GQ, P_GK, P_GV = 2816, 3072, 3200
P_TOTAL = 3328
S_TOTAL = 1536

Z_TOTAL = 2560
VMEM_LIMIT = 56 * 1024 * 1024


def _mm(a, b):
    return jnp.dot(a, b, preferred_element_type=F32)


def _nt(a, b):
    return lax.dot_general(a, b, (((1,), (1,)), ((), ())), preferred_element_type=F32)


def _resident(shape, index_map):
    return pl.BlockSpec(shape, index_map, pipeline_mode=pl.Buffered(1))


def _params(*sem):
    return pltpu.CompilerParams(dimension_semantics=sem, vmem_limit_bytes=VMEM_LIMIT)


def _lane_iota(shape):
    return lax.broadcasted_iota(jnp.int32, shape, len(shape) - 1)


def _rms_rows(x):
    return x * lax.rsqrt(jnp.mean(x * x, axis=-1, keepdims=True) + EPS)


def _seg_norm(v, ones_bd, n):
    ms = _mm((v * v).astype(BF16), ones_bd) * (1.0 / n)
    return v * lax.rsqrt(ms + EPS)


def _mod_kernel(c_ref, w_ref, b_ref, o_ref):
    c = c_ref[...]
    s = c * jax.nn.sigmoid(c)
    o_ref[...] = _mm(s.astype(BF16), w_ref[...].astype(BF16)) + b_ref[...]


def _modulation(cond, w_mod, b_mod):
    depth = w_mod.shape[0]
    rows = cond.shape[0]
    nblk = N_MOD
    return pl.pallas_call(
        _mod_kernel,
        grid=(depth, nblk),
        in_specs=[
            pl.BlockSpec((rows, D_MODEL), lambda l, j: (0, 0)),
            pl.BlockSpec((None, D_MODEL, D_MODEL), lambda l, j: (l, 0, j)),
            pl.BlockSpec((None, 1, D_MODEL), lambda l, j: (l, 0, j)),
        ],
        out_specs=pl.BlockSpec((None, rows, D_MODEL), lambda l, j: (l, 0, j)),
        out_shape=jax.ShapeDtypeStruct((depth, rows, N_MOD * D_MODEL), F32),
        compiler_params=_params("arbitrary", "arbitrary"),
        name="modulation",
    )(cond, w_mod, b_mod.reshape(depth, 1, N_MOD * D_MODEL))


def _proj_kernel(rope, *refs):
    it = iter(refs)
    x_ref, mod_ref, g1_ref, wa_ref, ga_ref = (next(it) for _ in range(5))
    wqb_ref, gmq_ref, wkp_ref, wv_ref, gmk_ref = (next(it) for _ in range(5))
    s64_ref, s32_ref = next(it), next(it)
    if rope:
        p32_ref, p64_ref, pm_ref = next(it), next(it), next(it)
        cd_ref, sd_ref, cg_ref, sg_ref, cm_ref, sm_ref = (next(it) for _ in range(6))
    proj_ref = next(it)
    state_ref = None if rope else next(it)

    x = x_ref[...]
    mod = mod_ref[...]
    sh1, sc1 = mod[0:1], mod[1:2]
    h = _rms_rows(x) * g1_ref[...] * (1.0 + sc1) + sh1
    z = _mm(h.astype(BF16), wa_ref[...])
    ga = ga_ref[...]
    s64 = s64_ref[...]
    s32 = s32_ref[...]

    def seg(lo, w):
        return z[:, lo:lo + w]

    def gain(lo, w):
        return ga[:, lo:lo + w]

    def rot(v, perm, cos, sin):
        return v * cos + _mm(v.astype(BF16), perm) * sin

    naq = _seg_norm(seg(0, 256), s64, 64) * gain(0, 256)
    nak = _seg_norm(seg(256, 256), s64, 64) * gain(256, 256)
    nav = seg(512, 256)
    dfq = _seg_norm(seg(768, 256), s32, 32) * gain(768, 256)
    dfk = _seg_norm(seg(1024, 256), s32, 32) * gain(1024, 256)
    dfv = seg(1280, 256)
    cq = _rms_rows(seg(1536, 256)) * gain(1536, 256)
    ckv = _rms_rows(seg(1792, 128)) * gain(1792, 128)
    krb = seg(1920, 128)
    mq_pre = _mm(cq.astype(BF16), wqb_ref[...])
    ckv_b = ckv.astype(BF16)
    mk_pre = _mm(ckv_b, wkp_ref[...])
    mv = _mm(ckv_b, wv_ref[...])
    gq = _seg_norm(seg(2048, 256), s64, 64) * gain(2048, 256)
    gk = _seg_norm(seg(2304, 128), s64[:128, :128], 64) * gain(2304, 128)
    gv = seg(2432, 128)

    if not rope:
        state_ref[:, 0:256] = nak
        state_ref[:, 256:512] = nav
        state_ref[:, 512:768] = dfk
        state_ref[:, 768:1024] = dfv
        state_ref[:, 1024:1152] = ckv
        state_ref[:, 1152:1280] = krb
        state_ref[:, 1280:1408] = gk
        state_ref[:, 1408:1536] = gv
    else:
        p32, p64 = p32_ref[...], p64_ref[...]
        cd, sd, cg, sg = cd_ref[...], sd_ref[...], cg_ref[...], sg_ref[...]
        dfq = rot(dfq, p32, cd, sd)
        dfk = rot(dfk, p32, cd, sd)
        gq = rot(gq, p64, cg, sg)
        gk = rot(gk, p64[:128, :128], cg[:, :128], sg[:, :128])

    proj_ref[:, P_NAQ:P_NAQ + 256] = naq.astype(BF16)
    proj_ref[:, P_NAK:P_NAK + 256] = nak.astype(BF16)
    proj_ref[:, P_NAV:P_NAV + 256] = nav.astype(BF16)
    proj_ref[:, P_DFQ:P_DFQ + 256] = dfq.astype(BF16)
    proj_ref[:, P_DFK:P_DFK + 256] = dfk.astype(BF16)
    proj_ref[:, P_DFV:P_DFV + 256] = dfv.astype(BF16)
    gmq = gmq_ref[...]
    gmk = gmk_ref[...]
    for hd in range(4):
        lo = hd * MLA_HEAD_PAD
        qb = mq_pre[:, lo:lo + MLA_HEAD_PAD]
        qb = qb * lax.rsqrt(jnp.sum(qb * qb, axis=-1, keepdims=True) * (1.0 / MLA_QK) + EPS)
        qb = qb * gmq[:, lo:lo + MLA_HEAD_PAD]
        kb = mk_pre[:, lo:lo + MLA_HEAD_PAD] + krb
        kb = kb * lax.rsqrt(jnp.sum(kb * kb, axis=-1, keepdims=True) * (1.0 / MLA_QK) + EPS)
        kb = kb * gmk[:, lo:lo + MLA_HEAD_PAD]
        if rope:
            pm, cm, sm = pm_ref[...], cm_ref[...], sm_ref[...]
            qb = rot(qb, pm, cm, sm)
            kb = rot(kb, pm, cm, sm)
        proj_ref[:, P_MQ + lo:P_MQ + lo + MLA_HEAD_PAD] = qb.astype(BF16)
        proj_ref[:, P_MK + lo:P_MK + lo + MLA_HEAD_PAD] = kb.astype(BF16)
    proj_ref[:, P_MV:P_MV + 256] = mv.astype(BF16)
    proj_ref[:, P_GQ:P_GQ + 256] = gq.astype(BF16)
    proj_ref[:, P_GK:P_GK + 128] = gk.astype(BF16)
    proj_ref[:, P_GV:P_GV + 128] = gv.astype(BF16)


def _project(x, mods, wts, consts, layer, *, rope, seq, cond_row0, tm):
    t = x.shape[0]
    tiles_per_seq = seq // tm
    nb = t // seq
    l = layer

    def tok(i, b):
        return (b * tiles_per_seq + i, 0)

    def cst(i, b):
        return (0, 0)

    def lay(i, b):
        return (l, 0, 0)

    def modmap(i, b):
        return (l, (cond_row0 + b) if rope else cond_row0, 0, 0)

    in_specs = [
        pl.BlockSpec((tm, D_MODEL), tok),
        pl.BlockSpec((None, None, N_MOD, D_MODEL), modmap),
        pl.BlockSpec((None, 1, D_MODEL), lay),
        _resident((None, D_MODEL, Z_TOTAL), lay),
        pl.BlockSpec((None, 1, Z_TOTAL), lay),
        _resident((None, 256, 512), lay),
        pl.BlockSpec((None, 1, 512), lay),
        _resident((None, 128, 512), lay),
        _resident((None, 128, 256), lay),
        pl.BlockSpec((None, 1, 512), lay),
        _resident((256, 256), cst),
        _resident((256, 256), cst),
    ]
    args = [x, mods, wts["norm1_g"], wts["w_a"], wts["g_a"], wts["w_qb"], wts["g_mq"],
            wts["w_kp"], wts["w_v"], wts["g_mk"], consts["s64"], consts["s32"]]
    if rope:
        def pos(i, b):
            return (i, 0)
        in_specs += [_resident((256, 256), cst), _resident((256, 256), cst), _resident((128, 128), cst)]
        in_specs += [pl.BlockSpec((tm, 256), pos)] * 4 + [pl.BlockSpec((tm, 128), pos)] * 2
        args += [consts["p32"], consts["p64"], consts["pm"], consts["cos_d"], consts["sin_d"],
                 consts["cos_g"], consts["sin_g"], consts["cos_m"], consts["sin_m"]]
    out_specs = [pl.BlockSpec((tm, P_TOTAL), tok)]
    out_shape = [jax.ShapeDtypeStruct((t, P_TOTAL), BF16)]
    if not rope:
        out_specs.append(pl.BlockSpec((tm, S_TOTAL), tok))
        out_shape.append(jax.ShapeDtypeStruct((t, S_TOTAL), F32))
    return pl.pallas_call(
        functools.partial(_proj_kernel, rope),
        grid=(tiles_per_seq, nb),
        in_specs=in_specs,
        out_specs=out_specs,
        out_shape=out_shape,
        compiler_params=_params("arbitrary", "arbitrary"),
        name="project_latent" if rope else "project_context",
    )(*args)


_BRANCHES = {
    "heads64": dict(
        maps=[(0, 256, 64 * h, 64 * h + 64, 0, 256, 0, h) for h in range(4)],
        vmasks=[(64 * h, 64 * h + 64) for h in range(4)], n_acc=1, wv=256),
    "diff": dict(
        maps=[(0, 256, 32 * (2 * h + i), 32 * (2 * h + i) + 32, 0, 256, i, h) for h in range(4) for i in range(2)],
        vmasks=[(64 * h, 64 * h + 64) for h in range(4)], n_acc=2, wv=256),
    "mla": dict(
        maps=[(128 * h, 128, None, None, 128 * h, 128, 0, h) for h in range(4)],
        vmasks=[(64 * h, 64 * h + 64) for h in range(4)], n_acc=1, wv=256),
    "gqa": dict(
        maps=[(128 * g, 128, 64 * j, 64 * j + 64, 0, 128, g, j) for g in range(2) for j in range(2)],
        vmasks=[(0, 64), (64, 128)], n_acc=2, wv=128),
}


def _compose_lanes(vals, ranges, rows, width):
    lane = _lane_iota((rows, width))
    out = jnp.zeros((rows, width), F32)
    for v, (lo, hi) in zip(vals, ranges):
        out = jnp.where((lane >= lo) & (lane < hi), jnp.tile(v, (1, width // LANE)), out)
    return out


def _flash_kernel(kind, n_lat, n_ctx, kc, lam_init, *refs):
    br = _BRANCHES[kind]
    maps, vmasks, n_acc, wv = br["maps"], br["vmasks"], br["n_acc"], br["wv"]
    it = iter(refs)
    q_ref = next(it)
    srcs = []
    if n_lat:
        srcs.append((next(it), next(it), n_lat))
    if n_ctx:
        srcs.append((next(it), next(it), n_ctx))
    if kind == "diff":
        lq1_ref, lk1_ref, lq2_ref, lk2_ref, gsub_ref, s64_ref = (next(it) for _ in range(6))
    o_ref = next(it)
    vm_scr, acc_scr, m_scr, l_scr = (next(it) for _ in range(4))
    tq = q_ref.shape[0]

    @pl.when(pl.program_id(1) == 0)
    def _():
        off = 0
        for k_ref, v_ref, n in srcs:
            for c in range(n // kc):
                vch = v_ref[c * kc:(c + 1) * kc, :]
                lane = _lane_iota(vch.shape)
                for vi, (lo, hi) in enumerate(vmasks):
                    vm_scr[vi, off + c * kc:off + (c + 1) * kc, :] = jnp.where(
                        (lane >= lo) & (lane < hi), vch, jnp.zeros_like(vch))
            off += n

    acc_scr[...] = jnp.zeros(acc_scr.shape, F32)
    m_scr[...] = jnp.full(m_scr.shape, NEG, F32)
    l_scr[...] = jnp.zeros(l_scr.shape, F32)

    q = q_ref[...]
    qm = []
    for (qlo, qw, mlo, mhi, _, _, _, _) in maps:
        qj = q[:, qlo:qlo + qw]
        if mlo is not None:
            lane = _lane_iota(qj.shape)
            qj = jnp.where((lane >= mlo) & (lane < mhi), qj, jnp.zeros_like(qj))
        qm.append(qj)

    def chunk(kch, voff):
        pbs, alphas = [], []
        for j, (_, _, _, _, klo, kw, _, _) in enumerate(maps):
            s = _nt(qm[j], kch[:, klo:klo + kw])
            m_prev = m_scr[j]
            m_new = jnp.maximum(m_prev, jnp.max(s, axis=-1, keepdims=True))
            alpha = jnp.exp(m_prev - m_new)
            p = jnp.exp(s - jnp.tile(m_new, (1, kc // LANE)))
            l_scr[j] = alpha * l_scr[j] + jnp.sum(p, axis=-1, keepdims=True)
            m_scr[j] = m_new
            pbs.append(p.astype(BF16))
            alphas.append(alpha)
        for a in range(n_acc):
            js = [j for j, mp in enumerate(maps) if mp[6] == a]
            al = _compose_lanes([alphas[j] for j in js], [vmasks[maps[j][7]] for j in js], tq, wv)
            pv = None
            for j in js:
                d = _mm(pbs[j], vm_scr[maps[j][7], pl.ds(voff, kc), :])
                pv = d if pv is None else pv + d
            acc_scr[a] = acc_scr[a] * al + pv

    off = 0
    for k_ref, v_ref, n in srcs:
        nch = n // kc
        if nch == 1:
            chunk(k_ref[...], off)
        else:
            def body(c, carry, k_ref=k_ref, off=off):
                start = pl.multiple_of(c * kc, kc)
                chunk(k_ref[pl.ds(start, kc), :], off + start)
                return carry
            lax.fori_loop(0, nch, body, 0)
        off += n

    outs = []
    for a in range(n_acc):
        js = [j for j, mp in enumerate(maps) if mp[6] == a]
        inv = _compose_lanes([1.0 / l_scr[j] for j in js], [vmasks[maps[j][7]] for j in js], tq, wv)
        outs.append(acc_scr[a] * inv)
    if kind == "diff":
        lam = (jnp.exp(jnp.sum(lq1_ref[...] * lk1_ref[...], axis=-1, keepdims=True))
               - jnp.exp(jnp.sum(lq2_ref[...] * lk2_ref[...], axis=-1, keepdims=True)) + lam_init)
        o = outs[0] - lam * outs[1]
        o = _seg_norm(o, s64_ref[...], 64) * gsub_ref[...] * (1.0 - lam_init)
        o_ref[...] = o.astype(BF16)
    elif kind == "gqa":
        o_ref[:, 0:128] = outs[0].astype(BF16)
        o_ref[:, 128:256] = outs[1].astype(BF16)
    else:
        o_ref[...] = outs[0].astype(BF16)


_BRANCH_COLS = {
    "heads64": (P_NAQ, 256, P_NAK, 256, P_NAV, 256),
    "diff": (P_DFQ, 256, P_DFK, 256, P_DFV, 256),
    "mla": (P_MQ, 512, P_MK, 512, P_MV, 256),
    "gqa": (P_GQ, 256, P_GK, 128, P_GV, 128),
}


def _attend(kind, proj, seq, layer, *, cache_k=None, cache_v=None, diff_args=None, lam_init=0.0, tq=256, kc=None):
    t = proj.shape[0]
    nb = t // seq
    nq = seq // tq
    qo, qw, ko, kw, vo, vw = _BRANCH_COLS[kind]
    br = _BRANCHES[kind]
    n_ctx = 0 if cache_k is None else cache_k.shape[2]
    kc = kc or min(seq, 512)
    l = layer

    in_specs = [
        pl.BlockSpec((tq, qw), lambda b, i: (b * nq + i, qo // qw)),
        pl.BlockSpec((seq, kw), lambda b, i: (b, ko // kw)),
        pl.BlockSpec((seq, vw), lambda b, i: (b, vo // vw)),
    ]
    args = [proj, proj, proj]
    if n_ctx:
        in_specs += [
            pl.BlockSpec((None, None, n_ctx, kw), lambda b, i: (b, l, 0, 0)),
            pl.BlockSpec((None, None, n_ctx, vw), lambda b, i: (b, l, 0, 0)),
        ]
        args += [cache_k, cache_v]
    if kind == "diff":
        in_specs += [pl.BlockSpec((None, 1, DIFF_DH), lambda b, i: (l, 0, 0))] * 4
        in_specs += [pl.BlockSpec((None, 1, 256), lambda b, i: (l, 0, 0)),
                     pl.BlockSpec((256, 256), lambda b, i: (0, 0))]
        args += list(diff_args)
    n_maps = len(br["maps"])
    scratch = [
        pltpu.VMEM((len(br["vmasks"]), seq + n_ctx, vw), BF16),
        pltpu.VMEM((br["n_acc"], tq, vw), F32),
        pltpu.VMEM((n_maps, tq, LANE), F32),
        pltpu.VMEM((n_maps, tq, LANE), F32),
    ]
    return pl.pallas_call(
        functools.partial(_flash_kernel, kind, seq, n_ctx, kc, lam_init),
        grid=(nb, nq),
        in_specs=in_specs,
        out_specs=pl.BlockSpec((tq, 256), lambda b, i: (b * nq + i, 0)),
        out_shape=jax.ShapeDtypeStruct((t, 256), BF16),
        scratch_shapes=scratch,
        compiler_params=_params("arbitrary", "arbitrary"),
        name="attn_" + kind + ("_latent" if n_ctx else "_context"),
    )(*args)


def _na_kernel(rows, q_ref, k_ref, v_ref, kc_ref, vc_ref, bias_ref, o_ref):
    kh = NA_KH
    band = kh * GRID_W
    lane = _lane_iota((GRID_W, 256))
    head_masks = [(lane >= 64 * h) & (lane < 64 * h + 64) for h in range(4)]
    kctx = kc_ref[...]
    vctx = vc_ref[...]

    def body(r, carry):
        rs = jnp.clip(r - kh // 2, 0, rows - kh)
        pat = jnp.where(r < kh // 2, r, jnp.where(r > rows - kh // 2, r - (rows - kh), kh // 2))
        q = q_ref[pl.ds(pl.multiple_of(r * GRID_W, GRID_W), GRID_W), :]
        qs = jnp.concatenate([jnp.where(mk, q, jnp.zeros_like(q)) for mk in head_masks], axis=0)
        start = pl.multiple_of(rs * GRID_W, GRID_W)
        kb = k_ref[pl.ds(start, band), :]
        vb = v_ref[pl.ds(start, band), :]
        sb = _nt(qs, kb) + bias_ref[pat]
        sc = _nt(qs, kctx)
        m = jnp.maximum(jnp.max(sb, axis=-1, keepdims=True), jnp.max(sc, axis=-1, keepdims=True))
        pb = jnp.exp(sb - m)
        pc = jnp.exp(sc - m)
        den = jnp.sum(pb, axis=-1, keepdims=True) + jnp.sum(pc, axis=-1, keepdims=True)
        of = (_mm(pb.astype(BF16), vb) + _mm(pc.astype(BF16), vctx)) * (1.0 / den)
        o = jnp.zeros((GRID_W, 256), F32)
        for h in range(4):
            o = jnp.where(head_masks[h], of[GRID_W * h:GRID_W * (h + 1), :], o)
        o_ref[pl.ds(pl.multiple_of(r * GRID_W, GRID_W), GRID_W), :] = o.astype(BF16)
        return carry

    lax.fori_loop(0, rows, body, 0)


def _na_bias(rpb, rows):
    kh, kw = NA_KH, NA_KW
    col = jnp.arange(GRID_W)
    cs = jnp.clip(col - kw // 2, 0, GRID_W - kw)
    col_ok = (col[None, :] >= cs[:, None]) & (col[None, :] < cs[:, None] + kw)
    dc_idx = jnp.clip(col[None, :] - col[:, None], -(kw - 1), kw - 1) + (NA_KW - 1)
    pats = []
    pat_rows = list(range(kh // 2)) + [kh // 2] + list(range(rows - kh // 2 + 1, rows))
    for r in pat_rows:
        rs = min(max(r - kh // 2, 0), rows - kh)
        dr_idx = rs + jnp.arange(kh) - r + (NA_KH - 1)
        bias = rpb[:, dr_idx][:, :, dc_idx]
        bias = jnp.transpose(bias, (0, 2, 1, 3))
        bias = jnp.where(col_ok[None, :, None, :], bias, NEG)
        pats.append(bias.reshape(4 * GRID_W, kh * GRID_W))
    return jnp.stack(pats, axis=0).astype(F32)


def _neighborhood(proj, seq, layer, cache_k, cache_v, bias):
    t = proj.shape[0]
    nb = t // seq
    rows = seq // GRID_W
    n_ctx = cache_k.shape[2]
    l = layer
    npat = bias.shape[0]
    return pl.pallas_call(
        functools.partial(_na_kernel, rows),
        grid=(nb,),
        in_specs=[
            pl.BlockSpec((seq, 256), lambda b: (b, P_NAQ // 256)),
            pl.BlockSpec((seq, 256), lambda b: (b, P_NAK // 256)),
            pl.BlockSpec((seq, 256), lambda b: (b, P_NAV // 256)),
            pl.BlockSpec((None, None, n_ctx, 256), lambda b: (b, l, 0, 0)),
            pl.BlockSpec((None, None, n_ctx, 256), lambda b: (b, l, 0, 0)),
            _resident((npat, 4 * GRID_W, NA_KH * GRID_W), lambda b: (0, 0, 0)),
        ],
        out_specs=pl.BlockSpec((seq, 256), lambda b: (b, 0)),
        out_shape=jax.ShapeDtypeStruct((t, 256), BF16),
        compiler_params=_params("arbitrary"),
        name="attn_neighborhood_latent",
    )(proj, proj, proj, cache_k, cache_v, bias)


def _mla_cache_kernel(ckv_ref, kr_ref, wkp_ref, wv_ref, gmk_ref, k_ref, v_ref):
    ckv_b = ckv_ref[...].astype(BF16)
    mk_pre = _mm(ckv_b, wkp_ref[...])
    krb = kr_ref[...]
    gmk = gmk_ref[...]
    for hd in range(4):
        lo = hd * MLA_HEAD_PAD
        kb = mk_pre[:, lo:lo + MLA_HEAD_PAD] + krb
        kb = kb * lax.rsqrt(jnp.sum(kb * kb, axis=-1, keepdims=True) * (1.0 / MLA_QK) + EPS)
        k_ref[:, lo:lo + MLA_HEAD_PAD] = (kb * gmk[:, lo:lo + MLA_HEAD_PAD]).astype(BF16)
    v_ref[...] = _mm(ckv_b, wv_ref[...]).astype(BF16)


def _mla_cache(cache_ckv, cache_kr_blk, wts):
    nb, depth, n_ctx, _ = cache_ckv.shape
    return pl.pallas_call(
        _mla_cache_kernel,
        grid=(depth, nb),
        in_specs=[
            pl.BlockSpec((None, None, n_ctx, 128), lambda l, b: (b, l, 0, 0)),
            pl.BlockSpec((None, None, n_ctx, 128), lambda l, b: (b, l, 0, 0)),
            pl.BlockSpec((None, 128, 512), lambda l, b: (l, 0, 0)),
            pl.BlockSpec((None, 128, 256), lambda l, b: (l, 0, 0)),
            pl.BlockSpec((None, 1, 512), lambda l, b: (l, 0, 0)),
        ],
        out_specs=[
            pl.BlockSpec((None, None, n_ctx, 512), lambda l, b: (b, l, 0, 0)),
            pl.BlockSpec((None, None, n_ctx, 256), lambda l, b: (b, l, 0, 0)),
        ],
        out_shape=[jax.ShapeDtypeStruct((nb, depth, n_ctx, 512), BF16),
                   jax.ShapeDtypeStruct((nb, depth, n_ctx, 256), BF16)],
        compiler_params=_params("arbitrary", "arbitrary"),
        name="mla_cache_keys",
    )(cache_ckv, cache_kr_blk, wts["w_kp"], wts["w_v"], wts["g_mk"])


def _merge_kernel(x_ref, mod_ref, g1_ref, o0_ref, o1_ref, o2_ref, o3_ref, wg_ref, wb_ref, wo_ref, y_ref):
    x = x_ref[...]
    mod = mod_ref[...]
    sh1, sc1, ga1 = mod[0:1], mod[1:2], mod[2:3]
    hb = (_rms_rows(x) * g1_ref[...] * (1.0 + sc1) + sh1).astype(BF16)
    y = None
    for m, o_ref in enumerate((o0_ref, o1_ref, o2_ref, o3_ref)):
        gate = jax.nn.sigmoid(_mm(hb, wg_ref[m]))
        term = gate * _mm(o_ref[...], wb_ref[m])
        y = term if y is None else y + term
    y_ref[...] = x + ga1 * _mm(y.astype(BF16), wo_ref[...])


def _merge(x, mods, outs, wts, layer, *, seq, cond_row0, per_batch, tm):
    t = x.shape[0]
    tiles_per_seq = seq // tm
    l = layer

    def tok(i):
        return (i, 0)

    def modmap(i):
        return (l, (cond_row0 + i // tiles_per_seq) if per_batch else cond_row0, 0, 0)

    return pl.pallas_call(
        _merge_kernel,
        grid=(t // tm,),
        in_specs=[
            pl.BlockSpec((tm, D_MODEL), tok),
            pl.BlockSpec((None, None, N_MOD, D_MODEL), modmap),
            pl.BlockSpec((None, 1, D_MODEL), lambda i: (l, 0, 0)),
        ] + [pl.BlockSpec((tm, 256), tok)] * 4 + [
            _resident((None, 4, D_MODEL, D_MODEL), lambda i: (l, 0, 0, 0)),
            _resident((None, 4, 256, D_MODEL), lambda i: (l, 0, 0, 0)),
            _resident((None, D_MODEL, D_MODEL), lambda i: (l, 0, 0)),
        ],
        out_specs=pl.BlockSpec((tm, D_MODEL), tok),
        out_shape=jax.ShapeDtypeStruct((t, D_MODEL), F32),
        compiler_params=_params("arbitrary"),
        name="merge",
    )(x, mods, wts["norm1_g"], *outs, wts["w_g"], wts["w_b"], wts["w_o"])


def _split_bf16(v):
    hi = v.astype(BF16)
    lo = (v - hi.astype(F32)).astype(BF16)
    return hi, lo


def _moe_kernel(x_ref, mod_ref, g2_ref, wrh_ref, wrl_ref, w1_ref, w3_ref, w2_ref, ex_ref, y_ref):
    x = x_ref[...]
    mod = mod_ref[...]
    sh2, sc2, ga2 = mod[3:4], mod[4:5], mod[5:6]
    h = _rms_rows(x) * g2_ref[...] * (1.0 + sc2) + sh2
    hb, hl = _split_bf16(h)
    wrh, wrl = wrh_ref[...], wrl_ref[...]
    logits = _mm(hb, wrh) + (_mm(hb, wrl) + _mm(hl, wrh))
    lane = _lane_iota(logits.shape).astype(F32)
    big = jnp.float32(1e9)

    def first_argmax(v, vmax):
        return jnp.min(jnp.where(v == vmax, lane, big), axis=-1, keepdims=True)

    gl = jnp.where(lane < MOE_GROUPS, logits, NEG)
    gmax = jnp.max(gl, axis=-1, keepdims=True)
    g_top = 1.0 / jnp.sum(jnp.exp(gl - gmax), axis=-1, keepdims=True)
    g_idx = first_argmax(gl, gmax)
    e_lo = EXPERT_LANE0 + MOE_EPG * g_idx
    el = jnp.where((lane >= e_lo) & (lane < e_lo + MOE_EPG), logits, NEG)
    e1 = jnp.max(el, axis=-1, keepdims=True)
    i1 = first_argmax(el, e1)
    el2 = jnp.where(lane == i1, NEG, el)
    e2 = jnp.max(el2, axis=-1, keepdims=True)
    i2 = first_argmax(el2, e2)
    r = jnp.exp(e2 - e1)
    w_1 = 1.0 / (1.0 + r)
    gate = jnp.where(lane == i1, g_top * w_1, jnp.where(lane == i2, g_top * (r * w_1), 0.0))
    gate_hi, gate_lo = _split_bf16(gate)

    acc = None
    for c in range(MOE_CHUNKS):
        ex = ex_ref[c]
        gate_c = _mm(gate_hi, ex) + _mm(gate_lo, ex)
        a = _mm(hb, w1_ref[c])
        hid = (a * jax.nn.sigmoid(a)) * _mm(hb, w3_ref[c]) * gate_c
        d = _mm(hid.astype(BF16), w2_ref[c])
        acc = d if acc is None else acc + d
    y_ref[...] = x + ga2 * acc


def _moe(x, mods, wts, consts, layer, *, seq, cond_row0, per_batch, tm):
    t = x.shape[0]
    tiles_per_seq = seq // tm
    l = layer

    def tok(i):
        return (i, 0)

    def modmap(i):
        return (l, (cond_row0 + i // tiles_per_seq) if per_batch else cond_row0, 0, 0)

    return pl.pallas_call(
        _moe_kernel,
        grid=(t // tm,),
        in_specs=[
            pl.BlockSpec((tm, D_MODEL), tok),
            pl.BlockSpec((None, None, N_MOD, D_MODEL), modmap),
            pl.BlockSpec((None, 1, D_MODEL), lambda i: (l, 0, 0)),
            _resident((None, D_MODEL, LANE), lambda i: (l, 0, 0)),
            _resident((None, D_MODEL, LANE), lambda i: (l, 0, 0)),
            _resident((None, MOE_CHUNKS, D_MODEL, MOE_CHUNK_W), lambda i: (l, 0, 0, 0)),
            _resident((None, MOE_CHUNKS, D_MODEL, MOE_CHUNK_W), lambda i: (l, 0, 0, 0)),
            _resident((None, MOE_CHUNKS, MOE_CHUNK_W, D_MODEL), lambda i: (l, 0, 0, 0)),
            _resident((MOE_CHUNKS, LANE, MOE_CHUNK_W), lambda i: (0, 0, 0)),
        ],
        out_specs=pl.BlockSpec((tm, D_MODEL), tok),
        out_shape=jax.ShapeDtypeStruct((t, D_MODEL), F32),
        compiler_params=_params("arbitrary"),
        name="moe",
    )(x, mods, wts["norm2_g"], wts["w_r_hi"], wts["w_r_lo"], wts["w_1"], wts["w_3"], wts["w_2"], consts["expand"])


def _block_ones(n, seg):
    i = np.arange(n)
    return (i[:, None] // seg == i[None, :] // seg).astype(np.float32)


def _rope_perm(n, lo, hi, quarter):
    p = np.zeros((n, n), np.float32)
    for i in range(lo, hi):
        if (i - lo) % (2 * quarter) < quarter:
            p[i + quarter, i] = -1.0
        else:
            p[i - quarter, i] = 1.0
    return p


def _axial_tables(seq, rot_dim):
    tpos = jnp.arange(seq)
    row = (tpos // GRID_W).astype(F32)
    col = (tpos % GRID_W).astype(F32)
    half = rot_dim // 2
    freqs = ROPE_BASE ** (-jnp.arange(0, half, 2, dtype=F32) / half)
    ar, ac = row[:, None] * freqs, col[:, None] * freqs
    ang = jnp.concatenate([ar, ar, ac, ac], axis=-1)
    return jnp.cos(ang), jnp.sin(ang)


def _constants(seq):
    cd, sd = _axial_tables(seq, DIFF_DH)
    cg, sg = _axial_tables(seq, HEAD_DIM)
    cm32, sm32 = _axial_tables(seq, MLA_ROPE)
    ones = jnp.ones((seq, MLA_NOPE), F32)
    zeros = jnp.zeros((seq, MLA_NOPE), F32)
    pad1 = jnp.ones((seq, MLA_HEAD_PAD - MLA_QK), F32)
    pad0 = jnp.zeros((seq, MLA_HEAD_PAD - MLA_QK), F32)
    expand = np.zeros((MOE_CHUNKS, LANE, MOE_CHUNK_W), np.float32)
    for e in range(MOE_EXPERTS):
        c, j = divmod(e, MOE_EXPERTS // MOE_CHUNKS)
        expand[c, EXPERT_LANE0 + e, j * MOE_FF:(j + 1) * MOE_FF] = 1.0
    return dict(
        s64=jnp.asarray(_block_ones(256, 64), BF16),
        s32=jnp.asarray(_block_ones(256, 32), BF16),
        p32=jnp.asarray(_rope_perm(256, 0, 256, DIFF_DH // 4), BF16),
        p64=jnp.asarray(_rope_perm(256, 0, 256, HEAD_DIM // 4), BF16),
        pm=jnp.asarray(_rope_perm(128, MLA_NOPE, MLA_QK, MLA_ROPE // 4), BF16),
        cos_d=jnp.tile(cd, (1, 256 // DIFF_DH)), sin_d=jnp.tile(sd, (1, 256 // DIFF_DH)),
        cos_g=jnp.tile(cg, (1, 256 // HEAD_DIM)), sin_g=jnp.tile(sg, (1, 256 // HEAD_DIM)),
        cos_m=jnp.concatenate([ones, cm32, pad1], axis=-1),
        sin_m=jnp.concatenate([zeros, sm32, pad0], axis=-1),
        expand=jnp.asarray(expand, BF16),
    )


def _layout_weights(p):
    depth = p["w_in"].shape[0]
    sizes = (256, 256, 256, 256, 256, 256, 256, 128, 32, 256, 128, 128, 4 * D_MODEL)
    cuts = np.concatenate([[0], np.cumsum(sizes)])
    w_in = p["w_in"]

    def seg(i):
        return w_in[:, :, cuts[i]:cuts[i + 1]]

    perm = jnp.array([0, 2, 1, 3])
    gq_w = seg(9).reshape(depth, D_MODEL, 4, HEAD_DIM)[:, :, perm].reshape(depth, D_MODEL, 256)
    zpad = lambda n: jnp.zeros((depth, D_MODEL, n), F32)
    kr_blk = jnp.concatenate([zpad(MLA_NOPE), seg(8), zpad(MLA_HEAD_PAD - MLA_QK)], axis=-1)
    w_a = jnp.concatenate([seg(0), seg(1), seg(2), seg(3), seg(4), seg(5), seg(6), seg(7), kr_blk,
                           gq_w, seg(10), seg(11)], axis=-1).astype(BF16)
    w_g = jnp.transpose(seg(12).reshape(depth, D_MODEL, 4, D_MODEL), (0, 2, 1, 3)).astype(BF16)

    ones = lambda n: jnp.ones((depth, n), F32)
    tile = lambda g, n: jnp.tile(g, (1, n))
    g_a = jnp.concatenate([
        tile(p["na_q_g"], 4) * HEAD_DIM ** -0.5, tile(p["na_k_g"], 4), ones(256),
        tile(p["diff_q_g"], 8) * DIFF_DH ** -0.5, tile(p["diff_k_g"], 8), ones(256),
        p["mla_q_a_g"], p["mla_kv_a_g"], ones(128),
        tile(p["gqa_q_g"], 4) * HEAD_DIM ** -0.5, tile(p["gqa_k_g"], 2), ones(128)], axis=-1)[:, None, :]

    def pad_heads(g):
        return jnp.tile(jnp.pad(g, ((0, 0), (0, MLA_HEAD_PAD - MLA_QK))), (1, 4))[:, None, :]

    w_qb = jnp.pad(p["mla_w_q_b"].reshape(depth, 256, 4, MLA_QK),
                   ((0, 0), (0, 0), (0, 0), (0, MLA_HEAD_PAD - MLA_QK))).reshape(depth, 256, 512).astype(BF16)
    kvb = p["mla_w_kv_b"].reshape(depth, 128, 4, 128)
    w_kp = jnp.pad(kvb[..., :MLA_NOPE], ((0, 0), (0, 0), (0, 0), (0, MLA_HEAD_PAD - MLA_NOPE)))
    w_kp = w_kp.reshape(depth, 128, 512).astype(BF16)
    w_v = kvb[..., MLA_NOPE:].reshape(depth, 128, 256).astype(BF16)

    w_b = p["w_branch"]
    w_b3 = w_b[:, 3].reshape(depth, 4, HEAD_DIM, D_MODEL)[:, perm].reshape(depth, 256, D_MODEL)
    w_b = jnp.concatenate([w_b[:, :3], w_b3[:, None]], axis=1).astype(BF16)

    w_r = jnp.zeros((depth, D_MODEL, LANE), F32)
    w_r = w_r.at[:, :, :MOE_GROUPS].set(p["moe_w_group"])
    w_r = w_r.at[:, :, EXPERT_LANE0:EXPERT_LANE0 + MOE_EXPERTS].set(p["moe_w_expert"])
    w_r_hi = w_r.astype(BF16)
    w_r_lo = (w_r - w_r_hi.astype(F32)).astype(BF16)

    def chunk_cols(w):
        return jnp.transpose(w.reshape(depth, D_MODEL, MOE_CHUNKS, MOE_CHUNK_W), (0, 2, 1, 3)).astype(BF16)

    return dict(
        norm1_g=p["norm1_g"][:, None, :], norm2_g=p["norm2_g"][:, None, :],
        w_a=w_a, g_a=g_a, w_g=w_g, w_qb=w_qb, w_kp=w_kp, w_v=w_v,
        g_mq=pad_heads(p["mla_q_g"]) * MLA_QK ** -0.5, g_mk=pad_heads(p["mla_k_g"]),
        w_b=w_b, w_o=p["w_out"].astype(BF16),
        w_r_hi=w_r_hi, w_r_lo=w_r_lo,
        w_1=chunk_cols(p["moe_w1"]), w_3=chunk_cols(p["moe_w3"]),
        w_2=p["moe_w2"].reshape(depth, MOE_CHUNKS, MOE_CHUNK_W, D_MODEL).astype(BF16),
    )


def _pick_tile(n, pref):
    t = min(n, pref)
    while n % t:
        t //= 2
    return t


def kernel(x_prompt, x_sample, cache_na_k, cache_na_v, cache_diff_k, cache_diff_v, cache_mla_ckv, cache_mla_krope, cache_gqa_k, cache_gqa_v, c, c_ctx, w_mod, b_mod, norm1_g, norm2_g, w_in, na_q_g, na_k_g, na_rpb, diff_q_g, diff_k_g, diff_lq1, diff_lk1, diff_lq2, diff_lk2, diff_sub_g, mla_q_a_g, mla_w_q_b, mla_kv_a_g, mla_w_kv_b, mla_q_g, mla_k_g, gqa_q_g, gqa_k_g, w_branch, w_out, moe_w_group, moe_w_expert, moe_w1, moe_w3, moe_w2):
    nbp, seq_p, _ = x_prompt.shape
    nbs, seq_s, _ = x_sample.shape
    depth = w_in.shape[0]
    n_ctx = cache_na_k.shape[2]
    rows = seq_s // GRID_W

    wts = _layout_weights(dict(
        w_in=w_in, na_q_g=na_q_g, na_k_g=na_k_g, diff_q_g=diff_q_g, diff_k_g=diff_k_g,
        mla_q_a_g=mla_q_a_g, mla_kv_a_g=mla_kv_a_g, gqa_q_g=gqa_q_g, gqa_k_g=gqa_k_g,
        mla_w_q_b=mla_w_q_b, mla_w_kv_b=mla_w_kv_b, mla_q_g=mla_q_g, mla_k_g=mla_k_g,
        w_branch=w_branch, w_out=w_out, moe_w_group=moe_w_group, moe_w_expert=moe_w_expert,
        moe_w1=moe_w1, moe_w3=moe_w3, moe_w2=moe_w2, norm1_g=norm1_g, norm2_g=norm2_g))
    consts = _constants(seq_s)

    n_cond = nbs + 1
    cond = jnp.concatenate([c, c_ctx[None, :], jnp.zeros((-n_cond % 8, D_MODEL), F32)], axis=0)
    mods = _modulation(cond, w_mod, b_mod).reshape(depth, cond.shape[0], N_MOD, D_MODEL)

    flat = lambda a: a.reshape(a.shape[0], a.shape[1], a.shape[2], -1).astype(BF16)
    c_na_k, c_na_v = flat(cache_na_k), flat(cache_na_v)
    c_df_k, c_df_v = flat(cache_diff_k), flat(cache_diff_v)
    c_g_k, c_g_v = flat(cache_gqa_k), flat(cache_gqa_v)
    kr_blk = jnp.pad(cache_mla_krope, ((0, 0), (0, 0), (0, 0), (MLA_NOPE, MLA_HEAD_PAD - MLA_QK)))
    c_m_k, c_m_v = _mla_cache(cache_mla_ckv, kr_blk, wts)

    gsub = jnp.tile(diff_sub_g, (1, 4))[:, None, :]
    lam_args = [a[:, None, :] for a in (diff_lq1, diff_lk1, diff_lq2, diff_lk2)]

    xp = x_prompt.reshape(nbp * seq_p, D_MODEL)
    xs = x_sample.reshape(nbs * seq_s, D_MODEL)
    tm_p = _pick_tile(seq_p, 512)
    tm_s = _pick_tile(seq_s, 512)
    states = []
    for l in range(depth):
        lam_init = 0.8 - 0.6 * math.exp(-0.3 * l)
        diff_args = lam_args + [gsub, consts["s64"]]
        bias = _na_bias(na_rpb[l], rows)

        proj, state = _project(xp, mods, wts, consts, l, rope=False, seq=seq_p, cond_row0=nbs, tm=tm_p)
        outs = [
            _attend("heads64", proj, seq_p, l, tq=tm_p),
            _attend("diff", proj, seq_p, l, diff_args=diff_args, lam_init=lam_init, tq=tm_p),
            _attend("mla", proj, seq_p, l, tq=tm_p),
            _attend("gqa", proj, seq_p, l, tq=tm_p),
        ]
        xp = _merge(xp, mods, outs, wts, l, seq=seq_p, cond_row0=nbs, per_batch=False, tm=tm_p)
        xp = _moe(xp, mods, wts, consts, l, seq=seq_p, cond_row0=nbs, per_batch=False, tm=tm_p)
        states.append(state.reshape(nbp, seq_p, S_TOTAL))

        proj = _project(xs, mods, wts, consts, l, rope=True, seq=seq_s, cond_row0=0, tm=tm_s)[0]
        outs = [
            _neighborhood(proj, seq_s, l, c_na_k, c_na_v, bias),
            _attend("diff", proj, seq_s, l, cache_k=c_df_k, cache_v=c_df_v, diff_args=diff_args, lam_init=lam_init),
            _attend("mla", proj, seq_s, l, cache_k=c_m_k, cache_v=c_m_v),
            _attend("gqa", proj, seq_s, l, cache_k=c_g_k, cache_v=c_g_v),
        ]
        xs = _merge(xs, mods, outs, wts, l, seq=seq_s, cond_row0=0, per_batch=True, tm=tm_s)
        xs = _moe(xs, mods, wts, consts, l, seq=seq_s, cond_row0=0, per_batch=True, tm=tm_s)

    st = jnp.stack(states, axis=1)
    heads = lambda a, h: a.reshape(nbp, depth, seq_p, h, HEAD_DIM)
    return (
        xp.reshape(nbp, seq_p, D_MODEL),
        xs.reshape(nbs, seq_s, D_MODEL),
        heads(st[..., 0:256], 4), heads(st[..., 256:512], 4),
        heads(st[..., 512:768], 4), heads(st[..., 768:1024], 4),
        st[..., 1024:1152], st[..., 1152 + MLA_NOPE:1152 + MLA_QK],
        heads(st[..., 1280:1408], 2), heads(st[..., 1408:1536], 2),
    )
```

```python
import functools
import math

import numpy as np
import jax
import jax.numpy as jnp
from jax import lax
from jax.experimental import pallas as pl
from jax.experimental.pallas import tpu as pltpu

F32 = jnp.float32
BF16 = jnp.bfloat16

D_MODEL = 1024
GRID_W = 64
HEAD_DIM = 64
NA_KH = 8
NA_KW = 16
DIFF_DH = 32
MLA_HEAD_PAD = 128
MLA_QK = 96
MLA_NOPE = 64
MLA_ROPE = 32
MOE_GROUPS = 4
MOE_EPG = 8
MOE_EXPERTS = 32
MOE_FF = 128
N_MOD = 6
ROPE_BASE = 10000.0
EPS = 1e-6
NEG = -1e30
LANE = 128
EXPERT_LANE0 = 32
MOE_CHUNKS = 8
MOE_CHUNK_W = MOE_EXPERTS * MOE_FF // MOE_CHUNKS

P_NAQ, P_NAK, P_NAV = 0, 256, 512
P_DFQ, P_DFK, P_GQ = 768, 1024, 1280
P_MQ, P_MK, P_GK = 1536, 2048, 2560
P_TOTAL = 2688
VT_ROWS_LATENT = 640
VT_ROWS_CONTEXT = 896
LOG2E = math.log2(math.e)
S_TOTAL = 1536

Z_TOTAL = 2560
VMEM_LIMIT = 56 * 1024 * 1024


def _mm(a, b):
    return jnp.dot(a, b, preferred_element_type=F32)


def _nt(a, b):
    return lax.dot_general(a, b, (((1,), (1,)), ((), ())), preferred_element_type=F32)


def _resident(shape, index_map):
    return pl.BlockSpec(shape, index_map, pipeline_mode=pl.Buffered(1))


def _params(*sem):
    return pltpu.CompilerParams(dimension_semantics=sem, vmem_limit_bytes=VMEM_LIMIT)


def _lane_iota(shape):
    return lax.broadcasted_iota(jnp.int32, shape, len(shape) - 1)


def _rms_rows(x):
    return x * lax.rsqrt(jnp.mean(x * x, axis=-1, keepdims=True) + EPS)


def _seg_norm(v, ones_bd, n):
    ms = _mm((v * v).astype(BF16), ones_bd) * (1.0 / n)
    return v * lax.rsqrt(ms + EPS)


def _mod_kernel(c_ref, w_ref, b_ref, o_ref):
    c = c_ref[...]
    s = c * jax.nn.sigmoid(c)
    o_ref[...] = _mm(s.astype(BF16), w_ref[...].astype(BF16)) + b_ref[...]


def _modulation(cond, w_mod, b_mod):
    depth = w_mod.shape[0]
    rows = cond.shape[0]
    nblk = N_MOD
    return pl.pallas_call(
        _mod_kernel,
        grid=(depth, nblk),
        in_specs=[
            pl.BlockSpec((rows, D_MODEL), lambda l, j: (0, 0)),
            pl.BlockSpec((None, D_MODEL, D_MODEL), lambda l, j: (l, 0, j)),
            pl.BlockSpec((None, 1, D_MODEL), lambda l, j: (l, 0, j)),
        ],
        out_specs=pl.BlockSpec((None, rows, D_MODEL), lambda l, j: (l, 0, j)),
        out_shape=jax.ShapeDtypeStruct((depth, rows, N_MOD * D_MODEL), F32),
        compiler_params=_params("arbitrary", "arbitrary"),
        name="modulation",
    )(cond, w_mod, b_mod.reshape(depth, 1, N_MOD * D_MODEL))


def _proj_kernel(rope, *refs):
    it = iter(refs)
    x_ref, mod_ref, g1_ref, wa_ref, ga_ref = (next(it) for _ in range(5))
    wqb_ref, gmq_ref, wkp_ref, wv_ref, gmk_ref = (next(it) for _ in range(5))
    s64_ref, s32_ref = next(it), next(it)
    if rope:
        p32_ref, p64_ref, pm_ref = next(it), next(it), next(it)
        cd_ref, sd_ref, cg_ref, sg_ref, cm_ref, sm_ref = (next(it) for _ in range(6))
    proj_ref = next(it)
    vt_ref = next(it)
    state_ref = None if rope else next(it)

    x = x_ref[...]
    mod = mod_ref[...]
    sh1, sc1 = mod[0:1], mod[1:2]
    h = _rms_rows(x) * g1_ref[...] * (1.0 + sc1) + sh1
    z = _mm(h.astype(BF16), wa_ref[...])
    ga = ga_ref[...]
    s64 = s64_ref[...]
    s32 = s32_ref[...]

    def seg(lo, w):
        return z[:, lo:lo + w]

    def gain(lo, w):
        return ga[:, lo:lo + w]

    def rot(v, perm, cos, sin):
        return v * cos + _mm(v.astype(BF16), perm) * sin

    naq = _seg_norm(seg(0, 256), s64, 64) * gain(0, 256)
    nak = _seg_norm(seg(256, 256), s64, 64) * gain(256, 256)
    nav = seg(512, 256)
    dfq = _seg_norm(seg(768, 256), s32, 32) * gain(768, 256)
    dfk = _seg_norm(seg(1024, 256), s32, 32) * gain(1024, 256)
    dfv = seg(1280, 256)
    cq = _rms_rows(seg(1536, 256)) * gain(1536, 256)
    ckv = _rms_rows(seg(1792, 128)) * gain(1792, 128)
    krb = seg(1920, 128)
    mq_pre = _mm(cq.astype(BF16), wqb_ref[...])
    ckv_b = ckv.astype(BF16)
    mk_pre = _mm(ckv_b, wkp_ref[...])
    mv = _mm(ckv_b, wv_ref[...])
    gq = _seg_norm(seg(2048, 256), s64, 64) * gain(2048, 256)
    gk = _seg_norm(seg(2304, 128), s64[:128, :128], 64) * gain(2304, 128)
    gv = seg(2432, 128)

    if not rope:
        state_ref[:, 0:256] = nak
        state_ref[:, 256:512] = nav
        state_ref[:, 512:768] = dfk
        state_ref[:, 768:1024] = dfv
        state_ref[:, 1024:1152] = ckv
        state_ref[:, 1152:1280] = krb
        state_ref[:, 1280:1408] = gk
        state_ref[:, 1408:1536] = gv
    else:
        p32, p64 = p32_ref[...], p64_ref[...]
        cd, sd, cg, sg = cd_ref[...], sd_ref[...], cg_ref[...], sg_ref[...]
        dfq = rot(dfq, p32, cd, sd)
        dfk = rot(dfk, p32, cd, sd)
        gq = rot(gq, p64, cg, sg)
        gk = rot(gk, p64[:128, :128], cg[:, :128], sg[:, :128])

    proj_ref[:, P_NAQ:P_NAQ + 256] = naq.astype(BF16)
    proj_ref[:, P_NAK:P_NAK + 256] = nak.astype(BF16)
    proj_ref[:, P_NAV:P_NAV + 256] = nav.astype(BF16)
    proj_ref[:, P_DFQ:P_DFQ + 256] = dfq.astype(BF16)
    proj_ref[:, P_DFK:P_DFK + 256] = dfk.astype(BF16)
    gmq = gmq_ref[...]
    gmk = gmk_ref[...]
    for hd in range(4):
        lo = hd * MLA_HEAD_PAD
        qb = mq_pre[:, lo:lo + MLA_HEAD_PAD]
        qb = qb * lax.rsqrt(jnp.sum(qb * qb, axis=-1, keepdims=True) * (1.0 / MLA_QK) + EPS)
        qb = qb * gmq[:, lo:lo + MLA_HEAD_PAD]
        kb = mk_pre[:, lo:lo + MLA_HEAD_PAD] + krb
        kb = kb * lax.rsqrt(jnp.sum(kb * kb, axis=-1, keepdims=True) * (1.0 / MLA_QK) + EPS)
        kb = kb * gmk[:, lo:lo + MLA_HEAD_PAD]
        if rope:
            pm, cm, sm = pm_ref[...], cm_ref[...], sm_ref[...]
            qb = rot(qb, pm, cm, sm)
            kb = rot(kb, pm, cm, sm)
        proj_ref[:, P_MQ + lo:P_MQ + lo + MLA_HEAD_PAD] = qb.astype(BF16)
        proj_ref[:, P_MK + lo:P_MK + lo + MLA_HEAD_PAD] = kb.astype(BF16)
    proj_ref[:, P_GQ:P_GQ + 256] = gq.astype(BF16)
    proj_ref[:, P_GK:P_GK + 128] = gk.astype(BF16)
    base = 0
    if not rope:
        vt_ref[0:256, :] = nav.T.astype(BF16)
        base = 256
    vt_ref[base:base + 256, :] = dfv.T.astype(BF16)
    vt_ref[base + 256:base + 512, :] = mv.T.astype(BF16)
    vt_ref[base + 512:base + 640, :] = gv.T.astype(BF16)


def _project(x, mods, wts, consts, layer, *, rope, seq, cond_row0, tm):
    t = x.shape[0]
    tiles_per_seq = seq // tm
    nb = t // seq
    l = layer

    def tok(i, b):
        return (b * tiles_per_seq + i, 0)

    def cst(i, b):
        return (0, 0)

    def lay(i, b):
        return (l, 0, 0)

    def modmap(i, b):
        return (l, (cond_row0 + b) if rope else cond_row0, 0, 0)

    in_specs = [
        pl.BlockSpec((tm, D_MODEL), tok),
        pl.BlockSpec((None, None, N_MOD, D_MODEL), modmap),
        pl.BlockSpec((None, 1, D_MODEL), lay),
        _resident((None, D_MODEL, Z_TOTAL), lay),
        pl.BlockSpec((None, 1, Z_TOTAL), lay),
        _resident((None, 256, 512), lay),
        pl.BlockSpec((None, 1, 512), lay),
        _resident((None, 128, 512), lay),
        _resident((None, 128, 256), lay),
        pl.BlockSpec((None, 1, 512), lay),
        _resident((256, 256), cst),
        _resident((256, 256), cst),
    ]
    args = [x, mods, wts["norm1_g"], wts["w_a"], wts["g_a"], wts["w_qb"], wts["g_mq"],
            wts["w_kp"], wts["w_v"], wts["g_mk"], consts["s64"], consts["s32"]]
    if rope:
        def pos(i, b):
            return (i, 0)
        in_specs += [_resident((256, 256), cst), _resident((256, 256), cst), _resident((128, 128), cst)]
        in_specs += [pl.BlockSpec((tm, 256), pos)] * 4 + [pl.BlockSpec((tm, 128), pos)] * 2
        args += [consts["p32"], consts["p64"], consts["pm"], consts["cos_d"], consts["sin_d"],
                 consts["cos_g"], consts["sin_g"], consts["cos_m"], consts["sin_m"]]
    vt_rows = VT_ROWS_LATENT if rope else VT_ROWS_CONTEXT
    out_specs = [pl.BlockSpec((tm, P_TOTAL), tok),
                 pl.BlockSpec((None, None, vt_rows, tm), lambda i, b: (b, i, 0, 0))]
    out_shape = [jax.ShapeDtypeStruct((t, P_TOTAL), BF16),
                 jax.ShapeDtypeStruct((nb, tiles_per_seq, vt_rows, tm), BF16)]
    if not rope:
        out_specs.append(pl.BlockSpec((tm, S_TOTAL), tok))
        out_shape.append(jax.ShapeDtypeStruct((t, S_TOTAL), F32))
    return pl.pallas_call(
        functools.partial(_proj_kernel, rope),
        grid=(tiles_per_seq, nb),
        in_specs=in_specs,
        out_specs=out_specs,
        out_shape=out_shape,
        compiler_params=_params("arbitrary", "arbitrary"),
        name="project_latent" if rope else "project_context",
    )(*args)


_BRANCHES = {
    "heads64": [(0, 256, 64 * h, 64 * h + 64, 0, 256, 64 * h) for h in range(4)],
    "diff": [(0, 256, 32 * (2 * h + i), 32 * (2 * h + i) + 32, 0, 256, 64 * h) for h in range(4) for i in range(2)],
    "mla": [(128 * h, 128, None, None, 128 * h, 128, 64 * h) for h in range(4)],
    "gqa": [(128 * g, 128, 64 * j, 64 * j + 64, 0, 128, 64 * j) for g in range(2) for j in range(2)],
}
ONES_ROWS = 16
ACC_ROWS = HEAD_DIM + ONES_ROWS
SCORE_LOOKAHEAD = 2
CHUNK_TILES = 4
LAZY_MAX_HEADROOM = 64.0


def _flash_kernel(kind, n_chunks, has_ctx, lam_init, *refs):
    maps = _BRANCHES[kind]
    it = iter(refs)
    q_ref, k_ref, vt_ref = next(it), next(it), next(it)
    if has_ctx:
        kctx_ref, vtctx_ref = next(it), next(it)
    if kind == "diff":
        lq1_ref, lk1_ref, lq2_ref, lk2_ref, gsub_ref, s64_ref = (next(it) for _ in range(6))
    o_ref = next(it)
    tq = q_ref.shape[0]
    kc = vt_ref.shape[-1]

    q = q_ref[...]
    qm = []
    for (qlo, qw, mlo, mhi, _, _, _) in maps:
        qj = q[:, qlo:qlo + qw]
        if mlo is not None:
            lane = _lane_iota(qj.shape)
            qj = jnp.where((lane >= mlo) & (lane < mhi), qj, jnp.zeros_like(qj))
        qm.append(qj)

    def chunk(kch, vtch, state, lazy):
        per_map, risk = state
        ones = jnp.ones((ONES_ROWS, kch.shape[0]), BF16)

        def scores(j):
            klo, kw = maps[j][4], maps[j][5]
            return _nt(kch[:, klo:klo + kw], qm[j])

        new_maps = []
        pending = [scores(j) for j in range(min(SCORE_LOOKAHEAD, len(maps)))]
        for j, ((m_prev, acc), (_, _, _, _, _, _, vrow)) in enumerate(zip(per_map, maps)):
            st = pending.pop(0)
            if j + SCORE_LOOKAHEAD < len(maps):
                pending.append(scores(j + SCORE_LOOKAHEAD))
            vaug = jnp.concatenate([vtch[vrow:vrow + HEAD_DIM, :], ones], axis=0)
            cmax = jnp.max(st, axis=0, keepdims=True)
            m_new = jnp.maximum(m_prev, cmax)
            alpha = jnp.exp2(m_prev - m_new)
            if lazy:
                pt = jnp.exp2(st - m_prev).astype(BF16)
                acc = (acc + _mm(vaug, pt)) * alpha
                risk = jnp.maximum(risk, cmax - m_prev)
            else:
                pt = jnp.exp2(st - m_new).astype(BF16)
                acc = acc * alpha + _mm(vaug, pt)
            new_maps.append((m_new, acc))
        return tuple(new_maps), risk

    def finish(state):
        heads = [acc[0:HEAD_DIM] * (1.0 / acc[HEAD_DIM:HEAD_DIM + 1]) for _, acc in state[0]]
        if kind == "diff":
            lam = (jnp.exp(jnp.sum(lq1_ref[...] * lk1_ref[...], axis=-1, keepdims=True))
                   - jnp.exp(jnp.sum(lq2_ref[...] * lk2_ref[...], axis=-1, keepdims=True)) + lam_init)
            heads = [heads[2 * h] - lam * heads[2 * h + 1] for h in range(4)]
        o = jnp.concatenate(heads, axis=0).T
        if kind == "diff":
            o = _seg_norm(o, s64_ref[...], 64) * gsub_ref[...] * (1.0 - lam_init)
        o_ref[...] = o.astype(BF16)

    def sweep(state, lazy):
        g = math.gcd(CHUNK_TILES, n_chunks)
        span = g * kc

        def body(c, st):
            vtch = jnp.concatenate([vt_ref[c * g + i] for i in range(g)], axis=1)
            return chunk(k_ref[pl.ds(pl.multiple_of(c * span, span), span), :], vtch, st, lazy)
        return lax.fori_loop(0, n_chunks // g, body, state)

    state = (tuple((jnp.full((1, tq), NEG, F32), jnp.zeros((ACC_ROWS, tq), F32)) for _ in maps),
             jnp.zeros((1, tq), F32))
    if not has_ctx:
        assert n_chunks == 1
        finish(chunk(k_ref[...], vt_ref[0], state, False))
    else:
        state = chunk(kctx_ref[...], vtctx_ref[...], state, False)
        fast = sweep(state, True)
        unsafe = jnp.max(fast[1]) > LAZY_MAX_HEADROOM

        @pl.when(jnp.logical_not(unsafe))
        def _():
            finish(fast)

        @pl.when(unsafe)
        def _():
            finish(sweep(state, False))


_BRANCH_COLS = {
    "heads64": (P_NAQ, 256, P_NAK, 256, -256, 256),
    "diff": (P_DFQ, 256, P_DFK, 256, 0, 256),
    "mla": (P_MQ, 512, P_MK, 512, 256, 256),
    "gqa": (P_GQ, 256, P_GK, 128, 512, 128),
}


def _attend(kind, proj, vt, seq, layer, *, cache_k=None, cache_vt=None, diff_args=None, lam_init=0.0, tq=256):
    t = proj.shape[0]
    nb, n_chunks, vt_rows, kc = vt.shape
    nq = seq // tq
    qo, qw, ko, kw, vo, vw = _BRANCH_COLS[kind]
    vo += vt_rows - VT_ROWS_LATENT
    maps = _BRANCHES[kind]
    has_ctx = cache_k is not None
    l = layer

    in_specs = [
        pl.BlockSpec((tq, qw), lambda b, i: (b * nq + i, qo // qw)),
        pl.BlockSpec((seq, kw), lambda b, i: (b, ko // kw)),
        pl.BlockSpec((None, n_chunks, vw, kc), lambda b, i: (b, 0, vo // vw, 0)),
    ]
    args = [proj, proj, vt]
    if has_ctx:
        n_ctx = cache_k.shape[2]
        in_specs += [
            pl.BlockSpec((None, None, n_ctx, kw), lambda b, i: (b, l, 0, 0)),
            pl.BlockSpec((None, None, vw, n_ctx), lambda b, i: (b, l, 0, 0)),
        ]
        args += [cache_k, cache_vt]
    if kind == "diff":
        in_specs += [pl.BlockSpec((None, 1, DIFF_DH), lambda b, i: (l, 0, 0))] * 4
        in_specs += [pl.BlockSpec((None, 1, 256), lambda b, i: (l, 0, 0)),
                     pl.BlockSpec((256, 256), lambda b, i: (0, 0))]
        args += list(diff_args)
    return pl.pallas_call(
        functools.partial(_flash_kernel, kind, n_chunks, has_ctx, lam_init),
        grid=(nb, nq),
        in_specs=in_specs,
        out_specs=pl.BlockSpec((tq, 256), lambda b, i: (b * nq + i, 0)),
        out_shape=jax.ShapeDtypeStruct((t, 256), BF16),
        compiler_params=_params("arbitrary", "arbitrary"),
        name="attn_" + kind + ("_latent" if has_ctx else "_context"),
    )(*args)


def _na_kernel(rows, q_ref, k_ref, v_ref, kc_ref, vc_ref, bias_ref, o_ref):
    kh = NA_KH
    band = kh * GRID_W
    lane = _lane_iota((GRID_W, 256))
    head_masks = [(lane >= 64 * h) & (lane < 64 * h + 64) for h in range(4)]
    kctx = kc_ref[...]
    vctx = vc_ref[...]

    def body(r, carry):
        rs = jnp.clip(r - kh // 2, 0, rows - kh)
        pat = jnp.where(r < kh // 2, r, jnp.where(r > rows - kh // 2, r - (rows - kh), kh // 2))
        q = q_ref[pl.ds(pl.multiple_of(r * GRID_W, GRID_W), GRID_W), :]
        qs = jnp.concatenate([jnp.where(mk, q, jnp.zeros_like(q)) for mk in head_masks], axis=0)
        start = pl.multiple_of(rs * GRID_W, GRID_W)
        kb = k_ref[pl.ds(start, band), :]
        vb = v_ref[pl.ds(start, band), :]
        sb = _nt(qs, kb) + bias_ref[pat]
        sc = _nt(qs, kctx)
        m = jnp.maximum(jnp.max(sb, axis=-1, keepdims=True), jnp.max(sc, axis=-1, keepdims=True))
        pb = jnp.exp2(sb - m)
        pc = jnp.exp2(sc - m)
        den = jnp.sum(pb, axis=-1, keepdims=True) + jnp.sum(pc, axis=-1, keepdims=True)
        of = (_mm(pb.astype(BF16), vb) + _mm(pc.astype(BF16), vctx)) * (1.0 / den)
        o = jnp.zeros((GRID_W, 256), F32)
        for h in range(4):
            o = jnp.where(head_masks[h], of[GRID_W * h:GRID_W * (h + 1), :], o)
        o_ref[pl.ds(pl.multiple_of(r * GRID_W, GRID_W), GRID_W), :] = o.astype(BF16)
        return carry

    lax.fori_loop(0, rows, body, 0)


def _na_bias(rpb, rows):
    kh, kw = NA_KH, NA_KW
    depth = rpb.shape[0]
    col = np.arange(GRID_W)
    cs = np.clip(col - kw // 2, 0, GRID_W - kw)
    col_ok = (col[None, :] >= cs[:, None]) & (col[None, :] < cs[:, None] + kw)
    dc_idx = np.clip(col[None, :] - col[:, None], -(kw - 1), kw - 1) + (NA_KW - 1)
    by_col = jnp.take(rpb * LOG2E, jnp.asarray(dc_idx.reshape(-1)), axis=-1)
    by_col = by_col.reshape(depth, 4, 2 * NA_KH - 1, GRID_W, GRID_W)
    by_col = jnp.where(col_ok[None, None, None], by_col, NEG)
    pats = []
    pat_rows = list(range(kh // 2)) + [kh // 2] + list(range(rows - kh // 2 + 1, rows))
    for r in pat_rows:
        rs = min(max(r - kh // 2, 0), rows - kh)
        dr0 = rs - r + (NA_KH - 1)
        band = jnp.transpose(by_col[:, :, dr0:dr0 + kh], (0, 1, 3, 2, 4))
        pats.append(band.reshape(depth, 4 * GRID_W, kh * GRID_W))
    return jnp.stack(pats, axis=1)


def _neighborhood(proj, seq, layer, cache_k, cache_v, bias):
    t = proj.shape[0]
    nb = t // seq
    rows = seq // GRID_W
    n_ctx = cache_k.shape[2]
    l = layer
    npat = bias.shape[1]
    return pl.pallas_call(
        functools.partial(_na_kernel, rows),
        grid=(nb,),
        in_specs=[
            pl.BlockSpec((seq, 256), lambda b: (b, P_NAQ // 256)),
            pl.BlockSpec((seq, 256), lambda b: (b, P_NAK // 256)),
            pl.BlockSpec((seq, 256), lambda b: (b, P_NAV // 256)),
            pl.BlockSpec((None, None, n_ctx, 256), lambda b: (b, l, 0, 0)),
            pl.BlockSpec((None, None, n_ctx, 256), lambda b: (b, l, 0, 0)),
            _resident((None, npat, 4 * GRID_W, NA_KH * GRID_W), lambda b: (l, 0, 0, 0)),
        ],
        out_specs=pl.BlockSpec((seq, 256), lambda b: (b, 0)),
        out_shape=jax.ShapeDtypeStruct((t, 256), BF16),
        compiler_params=_params("arbitrary"),
        name="attn_neighborhood_latent",
    )(proj, proj, proj, cache_k, cache_v, bias)


def _mla_cache_kernel(ckv_ref, kr_ref, wkp_ref, wv_ref, gmk_ref, k_ref, v_ref):
    ckv_b = ckv_ref[...].astype(BF16)
    mk_pre = _mm(ckv_b, wkp_ref[...])
    krb = kr_ref[...]
    gmk = gmk_ref[...]
    for hd in range(4):
        lo = hd * MLA_HEAD_PAD
        kb = mk_pre[:, lo:lo + MLA_HEAD_PAD] + krb
        kb = kb * lax.rsqrt(jnp.sum(kb * kb, axis=-1, keepdims=True) * (1.0 / MLA_QK) + EPS)
        k_ref[:, lo:lo + MLA_HEAD_PAD] = (kb * gmk[:, lo:lo + MLA_HEAD_PAD]).astype(BF16)
    v_ref[...] = _mm(ckv_b, wv_ref[...]).T.astype(BF16)


def _mla_cache(cache_ckv, cache_kr_blk, wts):
    nb, depth, n_ctx, _ = cache_ckv.shape
    return pl.pallas_call(
        _mla_cache_kernel,
        grid=(depth, nb),
        in_specs=[
            pl.BlockSpec((None, None, n_ctx, 128), lambda l, b: (b, l, 0, 0)),
            pl.BlockSpec((None, None, n_ctx, 128), lambda l, b: (b, l, 0, 0)),
            pl.BlockSpec((None, 128, 512), lambda l, b: (l, 0, 0)),
            pl.BlockSpec((None, 128, 256), lambda l, b: (l, 0, 0)),
            pl.BlockSpec((None, 1, 512), lambda l, b: (l, 0, 0)),
        ],
        out_specs=[
            pl.BlockSpec((None, None, n_ctx, 512), lambda l, b: (b, l, 0, 0)),
            pl.BlockSpec((None, None, 256, n_ctx), lambda l, b: (b, l, 0, 0)),
        ],
        out_shape=[jax.ShapeDtypeStruct((nb, depth, n_ctx, 512), BF16),
                   jax.ShapeDtypeStruct((nb, depth, 256, n_ctx), BF16)],
        compiler_params=_params("arbitrary", "arbitrary"),
        name="mla_cache_keys",
    )(cache_ckv, cache_kr_blk, wts["w_kp"], wts["w_v"], wts["g_mk"])


def _merge_kernel(x_ref, mod_ref, g1_ref, o0_ref, o1_ref, o2_ref, o3_ref, wg_ref, wb_ref, wo_ref, y_ref):
    x = x_ref[...]
    mod = mod_ref[...]
    sh1, sc1, ga1 = mod[0:1], mod[1:2], mod[2:3]
    hb = (_rms_rows(x) * g1_ref[...] * (1.0 + sc1) + sh1).astype(BF16)
    y = None
    for m, o_ref in enumerate((o0_ref, o1_ref, o2_ref, o3_ref)):
        gate = jax.nn.sigmoid(_mm(hb, wg_ref[m]))
        term = gate * _mm(o_ref[...], wb_ref[m])
        y = term if y is None else y + term
    y_ref[...] = x + ga1 * _mm(y.astype(BF16), wo_ref[...])


def _merge(x, mods, outs, wts, layer, *, seq, cond_row0, per_batch, tm):
    t = x.shape[0]
    tiles_per_seq = seq // tm
    l = layer

    def tok(i):
        return (i, 0)

    def modmap(i):
        return (l, (cond_row0 + i // tiles_per_seq) if per_batch else cond_row0, 0, 0)

    return pl.pallas_call(
        _merge_kernel,
        grid=(t // tm,),
        in_specs=[
            pl.BlockSpec((tm, D_MODEL), tok),
            pl.BlockSpec((None, None, N_MOD, D_MODEL), modmap),
            pl.BlockSpec((None, 1, D_MODEL), lambda i: (l, 0, 0)),
        ] + [pl.BlockSpec((tm, 256), tok)] * 4 + [
            _resident((None, 4, D_MODEL, D_MODEL), lambda i: (l, 0, 0, 0)),
            _resident((None, 4, 256, D_MODEL), lambda i: (l, 0, 0, 0)),
            _resident((None, D_MODEL, D_MODEL), lambda i: (l, 0, 0)),
        ],
        out_specs=pl.BlockSpec((tm, D_MODEL), tok),
        out_shape=jax.ShapeDtypeStruct((t, D_MODEL), F32),
        compiler_params=_params("arbitrary"),
        name="merge",
    )(x, mods, wts["norm1_g"], *outs, wts["w_g"], wts["w_b"], wts["w_o"])


def _split_bf16(v):
    hi = v.astype(BF16)
    lo = (v - hi.astype(F32)).astype(BF16)
    return hi, lo


def _moe_kernel(x_ref, mod_ref, g2_ref, wr_ref, w1_ref, w3_ref, w2_ref, ex_ref, y_ref):
    x = x_ref[...]
    mod = mod_ref[...]
    sh2, sc2, ga2 = mod[3:4], mod[4:5], mod[5:6]
    h = _rms_rows(x) * g2_ref[...] * (1.0 + sc2) + sh2
    hb, hl = _split_bf16(h)
    lg = _mm(hb, wr_ref[...])
    logits = lg[:, :LANE] + (lg[:, LANE:] + _mm(hl, wr_ref[:, :LANE]))
    lane = _lane_iota(logits.shape).astype(F32)
    big = jnp.float32(1e9)

    def first_argmax(v, vmax):
        return jnp.min(jnp.where(v == vmax, lane, big), axis=-1, keepdims=True)

    gl = jnp.where(lane < MOE_GROUPS, logits, NEG)
    gmax = jnp.max(gl, axis=-1, keepdims=True)
    g_top = 1.0 / jnp.sum(jnp.exp(gl - gmax), axis=-1, keepdims=True)
    g_idx = first_argmax(gl, gmax)
    e_lo = EXPERT_LANE0 + MOE_EPG * g_idx
    el = jnp.where((lane >= e_lo) & (lane < e_lo + MOE_EPG), logits, NEG)
    e1 = jnp.max(el, axis=-1, keepdims=True)
    i1 = first_argmax(el, e1)
    el2 = jnp.where(lane == i1, NEG, el)
    e2 = jnp.max(el2, axis=-1, keepdims=True)
    i2 = first_argmax(el2, e2)
    r = jnp.exp(e2 - e1)
    w_1 = 1.0 / (1.0 + r)
    gate = jnp.where(lane == i1, g_top * w_1, jnp.where(lane == i2, g_top * (r * w_1), 0.0))
    gate_hl = jnp.concatenate(_split_bf16(gate), axis=1)

    acc = None
    for c in range(MOE_CHUNKS):
        gate_c = _mm(gate_hl, ex_ref[c])
        a = _mm(hb, w1_ref[c])
        hid = (a * jax.nn.sigmoid(a)) * _mm(hb, w3_ref[c]) * gate_c
        d = _mm(hid.astype(BF16), w2_ref[c])
        acc = d if acc is None else acc + d
    y_ref[...] = x + ga2 * acc


def _moe(x, mods, wts, consts, layer, *, seq, cond_row0, per_batch, tm):
    t = x.shape[0]
    tiles_per_seq = seq // tm
    l = layer

    def tok(i):
        return (i, 0)

    def modmap(i):
        return (l, (cond_row0 + i // tiles_per_seq) if per_batch else cond_row0, 0, 0)

    return pl.pallas_call(
        _moe_kernel,
        grid=(t // tm,),
        in_specs=[
            pl.BlockSpec((tm, D_MODEL), tok),
            pl.BlockSpec((None, None, N_MOD, D_MODEL), modmap),
            pl.BlockSpec((None, 1, D_MODEL), lambda i: (l, 0, 0)),
            _resident((None, D_MODEL, 2 * LANE), lambda i: (l, 0, 0)),
            _resident((None, MOE_CHUNKS, D_MODEL, MOE_CHUNK_W), lambda i: (l, 0, 0, 0)),
            _resident((None, MOE_CHUNKS, D_MODEL, MOE_CHUNK_W), lambda i: (l, 0, 0, 0)),
            _resident((None, MOE_CHUNKS, MOE_CHUNK_W, D_MODEL), lambda i: (l, 0, 0, 0)),
            _resident((MOE_CHUNKS, 2 * LANE, MOE_CHUNK_W), lambda i: (0, 0, 0)),
        ],
        out_specs=pl.BlockSpec((tm, D_MODEL), tok),
        out_shape=jax.ShapeDtypeStruct((t, D_MODEL), F32),
        compiler_params=_params("arbitrary"),
        name="moe",
    )(x, mods, wts["norm2_g"], wts["w_r"], wts["w_1"], wts["w_3"], wts["w_2"], consts["expand"])


def _block_ones(n, seg):
    i = np.arange(n)
    return (i[:, None] // seg == i[None, :] // seg).astype(np.float32)


def _rope_perm(n, lo, hi, quarter):
    p = np.zeros((n, n), np.float32)
    for i in range(lo, hi):
        if (i - lo) % (2 * quarter) < quarter:
            p[i + quarter, i] = -1.0
        else:
            p[i - quarter, i] = 1.0
    return p


def _axial_tables(seq, rot_dim):
    tpos = jnp.arange(seq)
    row = (tpos // GRID_W).astype(F32)
    col = (tpos % GRID_W).astype(F32)
    half = rot_dim // 2
    freqs = ROPE_BASE ** (-jnp.arange(0, half, 2, dtype=F32) / half)
    ar, ac = row[:, None] * freqs, col[:, None] * freqs
    ang = jnp.concatenate([ar, ar, ac, ac], axis=-1)
    return jnp.cos(ang), jnp.sin(ang)


def _constants(seq):
    cd, sd = _axial_tables(seq, DIFF_DH)
    cg, sg = _axial_tables(seq, HEAD_DIM)
    cm32, sm32 = _axial_tables(seq, MLA_ROPE)
    ones = jnp.ones((seq, MLA_NOPE), F32)
    zeros = jnp.zeros((seq, MLA_NOPE), F32)
    pad1 = jnp.ones((seq, MLA_HEAD_PAD - MLA_QK), F32)
    pad0 = jnp.zeros((seq, MLA_HEAD_PAD - MLA_QK), F32)
    expand = np.zeros((MOE_CHUNKS, 2 * LANE, MOE_CHUNK_W), np.float32)
    for e in range(MOE_EXPERTS):
        c, j = divmod(e, MOE_EXPERTS // MOE_CHUNKS)
        expand[c, EXPERT_LANE0 + e, j * MOE_FF:(j + 1) * MOE_FF] = 1.0
        expand[c, LANE + EXPERT_LANE0 + e, j * MOE_FF:(j + 1) * MOE_FF] = 1.0
    return dict(
        s64=jnp.asarray(_block_ones(256, 64), BF16),
        s32=jnp.asarray(_block_ones(256, 32), BF16),
        p32=jnp.asarray(_rope_perm(256, 0, 256, DIFF_DH // 4), BF16),
        p64=jnp.asarray(_rope_perm(256, 0, 256, HEAD_DIM // 4), BF16),
        pm=jnp.asarray(_rope_perm(128, MLA_NOPE, MLA_QK, MLA_ROPE // 4), BF16),
        cos_d=jnp.tile(cd, (1, 256 // DIFF_DH)), sin_d=jnp.tile(sd, (1, 256 // DIFF_DH)),
        cos_g=jnp.tile(cg, (1, 256 // HEAD_DIM)), sin_g=jnp.tile(sg, (1, 256 // HEAD_DIM)),
        cos_m=jnp.concatenate([ones, cm32, pad1], axis=-1),
        sin_m=jnp.concatenate([zeros, sm32, pad0], axis=-1),
        expand=jnp.asarray(expand, BF16),
    )


def _layout_weights(p):
    depth = p["w_in"].shape[0]
    sizes = (256, 256, 256, 256, 256, 256, 256, 128, 32, 256, 128, 128, 4 * D_MODEL)
    cuts = np.concatenate([[0], np.cumsum(sizes)])
    w_in = p["w_in"]

    def seg(i):
        return w_in[:, :, cuts[i]:cuts[i + 1]]

    perm = jnp.array([0, 2, 1, 3])
    gq_w = seg(9).reshape(depth, D_MODEL, 4, HEAD_DIM)[:, :, perm].reshape(depth, D_MODEL, 256)
    zpad = lambda n: jnp.zeros((depth, D_MODEL, n), F32)
    kr_blk = jnp.concatenate([zpad(MLA_NOPE), seg(8), zpad(MLA_HEAD_PAD - MLA_QK)], axis=-1)
    w_a = jnp.concatenate([seg(0), seg(1), seg(2), seg(3), seg(4), seg(5), seg(6), seg(7), kr_blk,
                           gq_w, seg(10), seg(11)], axis=-1).astype(BF16)
    w_g = jnp.transpose(seg(12).reshape(depth, D_MODEL, 4, D_MODEL), (0, 2, 1, 3)).astype(BF16)

    ones = lambda n: jnp.ones((depth, n), F32)
    tile = lambda g, n: jnp.tile(g, (1, n))
    g_a = jnp.concatenate([
        tile(p["na_q_g"], 4) * (HEAD_DIM ** -0.5 * LOG2E), tile(p["na_k_g"], 4), ones(256),
        tile(p["diff_q_g"], 8) * (DIFF_DH ** -0.5 * LOG2E), tile(p["diff_k_g"], 8), ones(256),
        p["mla_q_a_g"], p["mla_kv_a_g"], ones(128),
        tile(p["gqa_q_g"], 4) * (HEAD_DIM ** -0.5 * LOG2E), tile(p["gqa_k_g"], 2), ones(128)], axis=-1)[:, None, :]

    def pad_heads(g):
        return jnp.tile(jnp.pad(g, ((0, 0), (0, MLA_HEAD_PAD - MLA_QK))), (1, 4))[:, None, :]

    w_qb = jnp.pad(p["mla_w_q_b"].reshape(depth, 256, 4, MLA_QK),
                   ((0, 0), (0, 0), (0, 0), (0, MLA_HEAD_PAD - MLA_QK))).reshape(depth, 256, 512).astype(BF16)
    kvb = p["mla_w_kv_b"].reshape(depth, 128, 4, 128)
    w_kp = jnp.pad(kvb[..., :MLA_NOPE], ((0, 0), (0, 0), (0, 0), (0, MLA_HEAD_PAD - MLA_NOPE)))
    w_kp = w_kp.reshape(depth, 128, 512).astype(BF16)
    w_v = kvb[..., MLA_NOPE:].reshape(depth, 128, 256).astype(BF16)

    w_b = p["w_branch"]
    w_b3 = w_b[:, 3].reshape(depth, 4, HEAD_DIM, D_MODEL)[:, perm].reshape(depth, 256, D_MODEL)
    w_b = jnp.concatenate([w_b[:, :3], w_b3[:, None]], axis=1).astype(BF16)

    w_r = jnp.zeros((depth, D_MODEL, LANE), F32)
    w_r = w_r.at[:, :, :MOE_GROUPS].set(p["moe_w_group"])
    w_r = w_r.at[:, :, EXPERT_LANE0:EXPERT_LANE0 + MOE_EXPERTS].set(p["moe_w_expert"])
    w_r_hi = w_r.astype(BF16)
    w_r_lo = (w_r - w_r_hi.astype(F32)).astype(BF16)
    w_r = jnp.concatenate([w_r_hi, w_r_lo], axis=-1)

    def chunk_cols(w):
        return jnp.transpose(w.reshape(depth, D_MODEL, MOE_CHUNKS, MOE_CHUNK_W), (0, 2, 1, 3)).astype(BF16)

    return dict(
        norm1_g=p["norm1_g"][:, None, :], norm2_g=p["norm2_g"][:, None, :],
        w_a=w_a, g_a=g_a, w_g=w_g, w_qb=w_qb, w_kp=w_kp, w_v=w_v,
        g_mq=pad_heads(p["mla_q_g"]) * (MLA_QK ** -0.5 * LOG2E), g_mk=pad_heads(p["mla_k_g"]),
        w_b=w_b, w_o=p["w_out"].astype(BF16),
        w_r=w_r,
        w_1=chunk_cols(p["moe_w1"]), w_3=chunk_cols(p["moe_w3"]),
        w_2=p["moe_w2"].reshape(depth, MOE_CHUNKS, MOE_CHUNK_W, D_MODEL).astype(BF16),
    )


def _pick_tile(n, pref):
    t = min(n, pref)
    while n % t:
        t //= 2
    return t


def kernel(x_prompt, x_sample, cache_na_k, cache_na_v, cache_diff_k, cache_diff_v, cache_mla_ckv, cache_mla_krope, cache_gqa_k, cache_gqa_v, c, c_ctx, w_mod, b_mod, norm1_g, norm2_g, w_in, na_q_g, na_k_g, na_rpb, diff_q_g, diff_k_g, diff_lq1, diff_lk1, diff_lq2, diff_lk2, diff_sub_g, mla_q_a_g, mla_w_q_b, mla_kv_a_g, mla_w_kv_b, mla_q_g, mla_k_g, gqa_q_g, gqa_k_g, w_branch, w_out, moe_w_group, moe_w_expert, moe_w1, moe_w3, moe_w2):
    nbp, seq_p, _ = x_prompt.shape
    nbs, seq_s, _ = x_sample.shape
    depth = w_in.shape[0]
    n_ctx = cache_na_k.shape[2]
    rows = seq_s // GRID_W

    wts = _layout_weights(dict(
        w_in=w_in, na_q_g=na_q_g, na_k_g=na_k_g, diff_q_g=diff_q_g, diff_k_g=diff_k_g,
        mla_q_a_g=mla_q_a_g, mla_kv_a_g=mla_kv_a_g, gqa_q_g=gqa_q_g, gqa_k_g=gqa_k_g,
        mla_w_q_b=mla_w_q_b, mla_w_kv_b=mla_w_kv_b, mla_q_g=mla_q_g, mla_k_g=mla_k_g,
        w_branch=w_branch, w_out=w_out, moe_w_group=moe_w_group, moe_w_expert=moe_w_expert,
        moe_w1=moe_w1, moe_w3=moe_w3, moe_w2=moe_w2, norm1_g=norm1_g, norm2_g=norm2_g))
    consts = _constants(seq_s)

    n_cond = nbs + 1
    cond = jnp.concatenate([c, c_ctx[None, :], jnp.zeros((-n_cond % 8, D_MODEL), F32)], axis=0)
    mods = _modulation(cond, w_mod, b_mod).reshape(depth, cond.shape[0], N_MOD, D_MODEL)

    flat = lambda a: a.reshape(a.shape[0], a.shape[1], a.shape[2], -1).astype(BF16)
    flat_t = lambda a: jnp.swapaxes(flat(a), 2, 3)
    c_na_k, c_na_v = flat(cache_na_k), flat(cache_na_v)
    c_df_k, c_df_vt = flat(cache_diff_k), flat_t(cache_diff_v)
    c_g_k, c_g_vt = flat(cache_gqa_k), flat_t(cache_gqa_v)
    kr_blk = jnp.pad(cache_mla_krope, ((0, 0), (0, 0), (0, 0), (MLA_NOPE, MLA_HEAD_PAD - MLA_QK)))
    c_m_k, c_m_vt = _mla_cache(cache_mla_ckv, kr_blk, wts)

    gsub = jnp.tile(diff_sub_g, (1, 4))[:, None, :]
    lam_args = [a[:, None, :] for a in (diff_lq1, diff_lk1, diff_lq2, diff_lk2)]

    xp = x_prompt.reshape(nbp * seq_p, D_MODEL)
    xs = x_sample.reshape(nbs * seq_s, D_MODEL)
    tm_p = _pick_tile(seq_p, 512)
    tm_s = _pick_tile(seq_s, 512)
    states = []
    bias = _na_bias(na_rpb, rows)
    for l in range(depth):
        lam_init = 0.8 - 0.6 * math.exp(-0.3 * l)
        diff_args = lam_args + [gsub, consts["s64"]]

        proj, vt, state = _project(xp, mods, wts, consts, l, rope=False, seq=seq_p, cond_row0=nbs, tm=tm_p)
        outs = [
            _attend("heads64", proj, vt, seq_p, l, tq=tm_p),
            _attend("diff", proj, vt, seq_p, l, diff_args=diff_args, lam_init=lam_init, tq=tm_p),
            _attend("mla", proj, vt, seq_p, l, tq=tm_p),
            _attend("gqa", proj, vt, seq_p, l, tq=tm_p),
        ]
        xp = _merge(xp, mods, outs, wts, l, seq=seq_p, cond_row0=nbs, per_batch=False, tm=tm_p)
        xp = _moe(xp, mods, wts, consts, l, seq=seq_p, cond_row0=nbs, per_batch=False, tm=tm_p)
        states.append(state.reshape(nbp, seq_p, S_TOTAL))

        proj, vt = _project(xs, mods, wts, consts, l, rope=True, seq=seq_s, cond_row0=0, tm=tm_s)
        outs = [
            _neighborhood(proj, seq_s, l, c_na_k, c_na_v, bias),
            _attend("diff", proj, vt, seq_s, l, cache_k=c_df_k, cache_vt=c_df_vt, diff_args=diff_args,
                    lam_init=lam_init),
            _attend("mla", proj, vt, seq_s, l, cache_k=c_m_k, cache_vt=c_m_vt),
            _attend("gqa", proj, vt, seq_s, l, cache_k=c_g_k, cache_vt=c_g_vt),
        ]
        xs = _merge(xs, mods, outs, wts, l, seq=seq_s, cond_row0=0, per_batch=True, tm=tm_s)
        xs = _moe(xs, mods, wts, consts, l, seq=seq_s, cond_row0=0, per_batch=True, tm=tm_s)

    st = jnp.stack(states, axis=1)
    heads = lambda a, h: a.reshape(nbp, depth, seq_p, h, HEAD_DIM)
    return (
        xp.reshape(nbp, seq_p, D_MODEL),
        xs.reshape(nbs, seq_s, D_MODEL),
        heads(st[..., 0:256], 4), heads(st[..., 256:512], 4),
        heads(st[..., 512:768], 4), heads(st[..., 768:1024], 4),
        st[..., 1024:1152], st[..., 1152 + MLA_NOPE:1152 + MLA_QK],
        heads(st[..., 1280:1408], 2), heads(st[..., 1408:1536], 2),
    )
```

```python
import functools
import math

import numpy as np
import jax
import jax.numpy as jnp
from jax import lax
from jax.experimental import pallas as pl
from jax.experimental.pallas import tpu as pltpu

F32 = jnp.float32
BF16 = jnp.bfloat16

D_MODEL = 1024
GRID_W = 64
HEAD_DIM = 64
NA_KH = 8
NA_KW = 16
DIFF_DH = 32
MLA_HEAD_PAD = 128
MLA_QK = 96
MLA_NOPE = 64
MLA_ROPE = 32
MOE_GROUPS = 4
MOE_EPG = 8
MOE_EXPERTS = 32
MOE_FF = 128
N_MOD = 6
ROPE_BASE = 10000.0
EPS = 1e-6
NEG = -1e30
LANE = 128
EXPERT_LANE0 = 32
MOE_CHUNKS = 8
MOE_CHUNK_W = MOE_EXPERTS * MOE_FF // MOE_CHUNKS

P_NAQ, P_NAK, P_NAV = 0, 256, 512
P_DFQ, P_DFK, P_GQ = 768, 1024, 1280
P_MQ, P_MK, P_GK = 1536, 2048, 2560
P_TOTAL = 2688
VT_ROWS_LATENT = 640
VT_ROWS_CONTEXT = 896
LOG2E = math.log2(math.e)
S_TOTAL = 1536

Z_TOTAL = 2560
VMEM_LIMIT = 56 * 1024 * 1024


def _mm(a, b):
    return jnp.dot(a, b, preferred_element_type=F32)


def _nt(a, b):
    return lax.dot_general(a, b, (((1,), (1,)), ((), ())), preferred_element_type=F32)


def _resident(shape, index_map):
    return pl.BlockSpec(shape, index_map, pipeline_mode=pl.Buffered(1))


def _params(*sem):
    return pltpu.CompilerParams(dimension_semantics=sem, vmem_limit_bytes=VMEM_LIMIT)


def _lane_iota(shape):
    return lax.broadcasted_iota(jnp.int32, shape, len(shape) - 1)


def _rms_rows(x):
    return x * lax.rsqrt(jnp.mean(x * x, axis=-1, keepdims=True) + EPS)


def _seg_norm(v, ones_bd, n):
    ms = _mm((v * v).astype(BF16), ones_bd) * (1.0 / n)
    return v * lax.rsqrt(ms + EPS)


def _mod_kernel(c_ref, w_ref, b_ref, o_ref):
    c = c_ref[...]
    s = c * jax.nn.sigmoid(c)
    o_ref[...] = _mm(s.astype(BF16), w_ref[...].astype(BF16)) + b_ref[...]


def _modulation(cond, w_mod, b_mod):
    depth = w_mod.shape[0]
    rows = cond.shape[0]
    nblk = N_MOD
    return pl.pallas_call(
        _mod_kernel,
        grid=(depth, nblk),
        in_specs=[
            pl.BlockSpec((rows, D_MODEL), lambda l, j: (0, 0)),
            pl.BlockSpec((None, D_MODEL, D_MODEL), lambda l, j: (l, 0, j)),
            pl.BlockSpec((None, 1, D_MODEL), lambda l, j: (l, 0, j)),
        ],
        out_specs=pl.BlockSpec((None, rows, D_MODEL), lambda l, j: (l, 0, j)),
        out_shape=jax.ShapeDtypeStruct((depth, rows, N_MOD * D_MODEL), F32),
        compiler_params=_params("arbitrary", "arbitrary"),
        name="modulation",
    )(cond, w_mod, b_mod.reshape(depth, 1, N_MOD * D_MODEL))


def _proj_kernel(rope, *refs):
    it = iter(refs)
    x_ref, mod_ref, g1_ref, wa_ref, ga_ref = (next(it) for _ in range(5))
    wqb_ref, gmq_ref, wkp_ref, wv_ref, gmk_ref = (next(it) for _ in range(5))
    s64_ref, s32_ref = next(it), next(it)
    if rope:
        p32_ref, p64_ref, pm_ref = next(it), next(it), next(it)
        cd_ref, sd_ref, cg_ref, sg_ref, cm_ref, sm_ref = (next(it) for _ in range(6))
    proj_ref = next(it)
    vt_ref = next(it)
    state_ref = None if rope else next(it)

    x = x_ref[...]
    mod = mod_ref[...]
    sh1, sc1 = mod[0:1], mod[1:2]
    h = _rms_rows(x) * g1_ref[...] * (1.0 + sc1) + sh1
    z = _mm(h.astype(BF16), wa_ref[...])
    ga = ga_ref[...]
    s64 = s64_ref[...]
    s32 = s32_ref[...]

    def seg(lo, w):
        return z[:, lo:lo + w]

    def gain(lo, w):
        return ga[:, lo:lo + w]

    def rot(v, perm, cos, sin):
        return v * cos + _mm(v.astype(BF16), perm) * sin

    naq = _seg_norm(seg(0, 256), s64, 64) * gain(0, 256)
    nak = _seg_norm(seg(256, 256), s64, 64) * gain(256, 256)
    nav = seg(512, 256)
    dfq = _seg_norm(seg(768, 256), s32, 32) * gain(768, 256)
    dfk = _seg_norm(seg(1024, 256), s32, 32) * gain(1024, 256)
    dfv = seg(1280, 256)
    cq = _rms_rows(seg(1536, 256)) * gain(1536, 256)
    ckv = _rms_rows(seg(1792, 128)) * gain(1792, 128)
    krb = seg(1920, 128)
    mq_pre = _mm(cq.astype(BF16), wqb_ref[...])
    ckv_b = ckv.astype(BF16)
    mk_pre = _mm(ckv_b, wkp_ref[...])
    mv = _mm(ckv_b, wv_ref[...])
    gq = _seg_norm(seg(2048, 256), s64, 64) * gain(2048, 256)
    gk = _seg_norm(seg(2304, 128), s64[:128, :128], 64) * gain(2304, 128)
    gv = seg(2432, 128)

    if not rope:
        state_ref[:, 0:256] = nak
        state_ref[:, 256:512] = nav
        state_ref[:, 512:768] = dfk
        state_ref[:, 768:1024] = dfv
        state_ref[:, 1024:1152] = ckv
        state_ref[:, 1152:1280] = krb
        state_ref[:, 1280:1408] = gk
        state_ref[:, 1408:1536] = gv
    else:
        p32, p64 = p32_ref[...], p64_ref[...]
        cd, sd, cg, sg = cd_ref[...], sd_ref[...], cg_ref[...], sg_ref[...]
        dfq = rot(dfq, p32, cd, sd)
        dfk = rot(dfk, p32, cd, sd)
        gq = rot(gq, p64, cg, sg)
        gk = rot(gk, p64[:128, :128], cg[:, :128], sg[:, :128])

    proj_ref[:, P_NAQ:P_NAQ + 256] = naq.astype(BF16)
    proj_ref[:, P_NAK:P_NAK + 256] = nak.astype(BF16)
    proj_ref[:, P_NAV:P_NAV + 256] = nav.astype(BF16)
    proj_ref[:, P_DFQ:P_DFQ + 256] = dfq.astype(BF16)
    proj_ref[:, P_DFK:P_DFK + 256] = dfk.astype(BF16)
    gmq = gmq_ref[...]
    gmk = gmk_ref[...]
    for hd in range(4):
        lo = hd * MLA_HEAD_PAD
        qb = mq_pre[:, lo:lo + MLA_HEAD_PAD]
        qb = qb * lax.rsqrt(jnp.sum(qb * qb, axis=-1, keepdims=True) * (1.0 / MLA_QK) + EPS)
        qb = qb * gmq[:, lo:lo + MLA_HEAD_PAD]
        kb = mk_pre[:, lo:lo + MLA_HEAD_PAD] + krb
        kb = kb * lax.rsqrt(jnp.sum(kb * kb, axis=-1, keepdims=True) * (1.0 / MLA_QK) + EPS)
        kb = kb * gmk[:, lo:lo + MLA_HEAD_PAD]
        if rope:
            pm, cm, sm = pm_ref[...], cm_ref[...], sm_ref[...]
            qb = rot(qb, pm, cm, sm)
            kb = rot(kb, pm, cm, sm)
        proj_ref[:, P_MQ + lo:P_MQ + lo + MLA_HEAD_PAD] = qb.astype(BF16)
        proj_ref[:, P_MK + lo:P_MK + lo + MLA_HEAD_PAD] = kb.astype(BF16)
    proj_ref[:, P_GQ:P_GQ + 256] = gq.astype(BF16)
    proj_ref[:, P_GK:P_GK + 128] = gk.astype(BF16)
    base = 0
    if not rope:
        vt_ref[0:256, :] = nav.T.astype(BF16)
        base = 256
    vt_ref[base:base + 256, :] = dfv.T.astype(BF16)
    vt_ref[base + 256:base + 512, :] = mv.T.astype(BF16)
    vt_ref[base + 512:base + 640, :] = gv.T.astype(BF16)


def _project(x, mods, wts, consts, layer, *, rope, seq, cond_row0, tm):
    t = x.shape[0]
    tiles_per_seq = seq // tm
    nb = t // seq
    l = layer

    def tok(i, b):
        return (b * tiles_per_seq + i, 0)

    def cst(i, b):
        return (0, 0)

    def lay(i, b):
        return (l, 0, 0)

    def modmap(i, b):
        return (l, (cond_row0 + b) if rope else cond_row0, 0, 0)

    in_specs = [
        pl.BlockSpec((tm, D_MODEL), tok),
        pl.BlockSpec((None, None, N_MOD, D_MODEL), modmap),
        pl.BlockSpec((None, 1, D_MODEL), lay),
        _resident((None, D_MODEL, Z_TOTAL), lay),
        pl.BlockSpec((None, 1, Z_TOTAL), lay),
        _resident((None, 256, 512), lay),
        pl.BlockSpec((None, 1, 512), lay),
        _resident((None, 128, 512), lay),
        _resident((None, 128, 256), lay),
        pl.BlockSpec((None, 1, 512), lay),
        _resident((256, 256), cst),
        _resident((256, 256), cst),
    ]
    args = [x, mods, wts["norm1_g"], wts["w_a"], wts["g_a"], wts["w_qb"], wts["g_mq"],
            wts["w_kp"], wts["w_v"], wts["g_mk"], consts["s64"], consts["s32"]]
    if rope:
        def pos(i, b):
            return (i, 0)
        in_specs += [_resident((256, 256), cst), _resident((256, 256), cst), _resident((128, 128), cst)]
        in_specs += [pl.BlockSpec((tm, 256), pos)] * 4 + [pl.BlockSpec((tm, 128), pos)] * 2
        args += [consts["p32"], consts["p64"], consts["pm"], consts["cos_d"], consts["sin_d"],
                 consts["cos_g"], consts["sin_g"], consts["cos_m"], consts["sin_m"]]
    vt_rows = VT_ROWS_LATENT if rope else VT_ROWS_CONTEXT
    out_specs = [pl.BlockSpec((tm, P_TOTAL), tok),
                 pl.BlockSpec((None, None, vt_rows, tm), lambda i, b: (b, i, 0, 0))]
    out_shape = [jax.ShapeDtypeStruct((t, P_TOTAL), BF16),
                 jax.ShapeDtypeStruct((nb, tiles_per_seq, vt_rows, tm), BF16)]
    if not rope:
        out_specs.append(pl.BlockSpec((tm, S_TOTAL), tok))
        out_shape.append(jax.ShapeDtypeStruct((t, S_TOTAL), F32))
    return pl.pallas_call(
        functools.partial(_proj_kernel, rope),
        grid=(tiles_per_seq, nb),
        in_specs=in_specs,
        out_specs=out_specs,
        out_shape=out_shape,
        compiler_params=_params("arbitrary", "arbitrary"),
        name="project_latent" if rope else "project_context",
    )(*args)


_BRANCHES = {
    "heads64": [(0, 256, 64 * h, 64 * h + 64, 0, 256, 64 * h) for h in range(4)],
    "diff": [(0, 256, 32 * (2 * h + i), 32 * (2 * h + i) + 32, 0, 256, 64 * h) for h in range(4) for i in range(2)],
    "mla": [(128 * h, 128, None, None, 128 * h, 128, 64 * h) for h in range(4)],
    "gqa": [(128 * g, 128, 64 * j, 64 * j + 64, 0, 128, 64 * j) for g in range(2) for j in range(2)],
}
ONES_ROWS = 16
ACC_ROWS = HEAD_DIM + ONES_ROWS
SCORE_LOOKAHEAD = 2
CHUNK_TILES = 8
LAZY_MAX_HEADROOM = 64.0


def _flash_kernel(kind, n_chunks, has_ctx, lam_init, *refs):
    maps = _BRANCHES[kind]
    it = iter(refs)
    q_ref, k_ref, vt_ref = next(it), next(it), next(it)
    if has_ctx:
        kctx_ref, vtctx_ref = next(it), next(it)
    if kind == "diff":
        lq1_ref, lk1_ref, lq2_ref, lk2_ref, gsub_ref, s64_ref = (next(it) for _ in range(6))
    o_ref = next(it)
    tq = q_ref.shape[0]
    kc = vt_ref.shape[-1]

    q = q_ref[...]
    qm = []
    for (qlo, qw, mlo, mhi, _, _, _) in maps:
        qj = q[:, qlo:qlo + qw]
        if mlo is not None:
            lane = _lane_iota(qj.shape)
            qj = jnp.where((lane >= mlo) & (lane < mhi), qj, jnp.zeros_like(qj))
        qm.append(qj)

    def chunk(kch, vtch, state, lazy):
        per_map, risk = state
        ones = jnp.ones((ONES_ROWS, kch.shape[0]), BF16)

        def scores(j):
            klo, kw = maps[j][4], maps[j][5]
            return _nt(kch[:, klo:klo + kw], qm[j])

        new_maps = []
        pending = [scores(j) for j in range(min(SCORE_LOOKAHEAD, len(maps)))]
        for j, ((m_prev, acc), (_, _, _, _, _, _, vrow)) in enumerate(zip(per_map, maps)):
            st = pending.pop(0)
            if j + SCORE_LOOKAHEAD < len(maps):
                pending.append(scores(j + SCORE_LOOKAHEAD))
            vaug = jnp.concatenate([vtch[vrow:vrow + HEAD_DIM, :], ones], axis=0)
            cmax = jnp.max(st, axis=0, keepdims=True)
            m_new = jnp.maximum(m_prev, cmax)
            alpha = jnp.exp2(m_prev - m_new)
            if lazy:
                pt = jnp.exp2(st - m_prev).astype(BF16)
                acc = (acc + _mm(vaug, pt)) * alpha
                risk = jnp.maximum(risk, cmax - m_prev)
            else:
                pt = jnp.exp2(st - m_new).astype(BF16)
                acc = acc * alpha + _mm(vaug, pt)
            new_maps.append((m_new, acc))
        return tuple(new_maps), risk

    def finish(state):
        heads = [acc[0:HEAD_DIM] * (1.0 / acc[HEAD_DIM:HEAD_DIM + 1]) for _, acc in state[0]]
        if kind == "diff":
            lam = (jnp.exp(jnp.sum(lq1_ref[...] * lk1_ref[...], axis=-1, keepdims=True))
                   - jnp.exp(jnp.sum(lq2_ref[...] * lk2_ref[...], axis=-1, keepdims=True)) + lam_init)
            heads = [heads[2 * h] - lam * heads[2 * h + 1] for h in range(4)]
        o = jnp.concatenate(heads, axis=0).T
        if kind == "diff":
            o = _seg_norm(o, s64_ref[...], 64) * gsub_ref[...] * (1.0 - lam_init)
        o_ref[...] = o.astype(BF16)

    def sweep(state, lazy):
        g = math.gcd(CHUNK_TILES, n_chunks)
        span = g * kc

        def body(c, st):
            vtch = jnp.concatenate([vt_ref[c * g + i] for i in range(g)], axis=1)
            return chunk(k_ref[pl.ds(pl.multiple_of(c * span, span), span), :], vtch, st, lazy)
        return lax.fori_loop(0, n_chunks // g, body, state)

    state = (tuple((jnp.full((1, tq), NEG, F32), jnp.zeros((ACC_ROWS, tq), F32)) for _ in maps),
             jnp.zeros((1, tq), F32))
    if not has_ctx:
        assert n_chunks == 1
        finish(chunk(k_ref[...], vt_ref[0], state, False))
    else:
        state = chunk(kctx_ref[...], vtctx_ref[...], state, False)
        fast = sweep(state, True)
        unsafe = jnp.max(fast[1]) > LAZY_MAX_HEADROOM

        @pl.when(jnp.logical_not(unsafe))
        def _():
            finish(fast)

        @pl.when(unsafe)
        def _():
            finish(sweep(state, False))


_BRANCH_COLS = {
    "heads64": (P_NAQ, 256, P_NAK, 256, -256, 256),
    "diff": (P_DFQ, 256, P_DFK, 256, 0, 256),
    "mla": (P_MQ, 512, P_MK, 512, 256, 256),
    "gqa": (P_GQ, 256, P_GK, 128, 512, 128),
}


def _attend(kind, proj, vt, seq, layer, *, cache_k=None, cache_vt=None, diff_args=None, lam_init=0.0, tq=256):
    t = proj.shape[0]
    nb, n_chunks, vt_rows, kc = vt.shape
    nq = seq // tq
    qo, qw, ko, kw, vo, vw = _BRANCH_COLS[kind]
    vo += vt_rows - VT_ROWS_LATENT
    maps = _BRANCHES[kind]
    has_ctx = cache_k is not None
    l = layer

    in_specs = [
        pl.BlockSpec((tq, qw), lambda b, i: (b * nq + i, qo // qw)),
        pl.BlockSpec((seq, kw), lambda b, i: (b, ko // kw)),
        pl.BlockSpec((None, n_chunks, vw, kc), lambda b, i: (b, 0, vo // vw, 0)),
    ]
    args = [proj, proj, vt]
    if has_ctx:
        n_ctx = cache_k.shape[2]
        in_specs += [
            pl.BlockSpec((None, None, n_ctx, kw), lambda b, i: (b, l, 0, 0)),
            pl.BlockSpec((None, None, vw, n_ctx), lambda b, i: (b, l, 0, 0)),
        ]
        args += [cache_k, cache_vt]
    if kind == "diff":
        in_specs += [pl.BlockSpec((None, 1, DIFF_DH), lambda b, i: (l, 0, 0))] * 4
        in_specs += [pl.BlockSpec((None, 1, 256), lambda b, i: (l, 0, 0)),
                     pl.BlockSpec((256, 256), lambda b, i: (0, 0))]
        args += list(diff_args)
    return pl.pallas_call(
        functools.partial(_flash_kernel, kind, n_chunks, has_ctx, lam_init),
        grid=(nb, nq),
        in_specs=in_specs,
        out_specs=pl.BlockSpec((tq, 256), lambda b, i: (b * nq + i, 0)),
        out_shape=jax.ShapeDtypeStruct((t, 256), BF16),
        compiler_params=_params("arbitrary", "arbitrary"),
        name="attn_" + kind + ("_latent" if has_ctx else "_context"),
    )(*args)


NA_ROWS_PER_STEP = 4
NA_LOOKAHEAD = 2


def _na_kernel(rows, q_ref, k_ref, v_ref, kc_ref, vc_ref, bias_ref, o_ref):
    kh = NA_KH
    band = kh * GRID_W
    lane = _lane_iota((GRID_W, 256))
    head_masks = [(lane >= 64 * h) & (lane < 64 * h + 64) for h in range(4)]
    kctx = kc_ref[...]
    vctx = vc_ref[...]

    def scores(r):
        rs = jnp.clip(r - kh // 2, 0, rows - kh)
        pat = jnp.where(r < kh // 2, r, jnp.where(r > rows - kh // 2, r - (rows - kh), kh // 2))
        q = q_ref[pl.ds(pl.multiple_of(r * GRID_W, GRID_W), GRID_W), :]
        qs = jnp.concatenate([jnp.where(mk, q, jnp.zeros_like(q)) for mk in head_masks], axis=0)
        start = pl.multiple_of(rs * GRID_W, GRID_W)
        sb = _nt(qs, k_ref[pl.ds(start, band), :]) + bias_ref[pat]
        sc = _nt(qs, kctx)
        return r, start, sb, sc

    def finish(r, start, sb, sc):
        m = jnp.maximum(jnp.max(sb, axis=-1, keepdims=True), jnp.max(sc, axis=-1, keepdims=True))
        pb = jnp.exp2(sb - m)
        pc = jnp.exp2(sc - m)
        den = jnp.sum(pb, axis=-1, keepdims=True) + jnp.sum(pc, axis=-1, keepdims=True)
        vb = v_ref[pl.ds(start, band), :]
        of = (_mm(pb.astype(BF16), vb) + _mm(pc.astype(BF16), vctx)) * (1.0 / den)
        o = jnp.zeros((GRID_W, 256), F32)
        for h in range(4):
            o = jnp.where(head_masks[h], of[GRID_W * h:GRID_W * (h + 1), :], o)
        o_ref[pl.ds(pl.multiple_of(r * GRID_W, GRID_W), GRID_W), :] = o.astype(BF16)

    def body(i, carry):
        todo = [i * NA_ROWS_PER_STEP + t for t in range(NA_ROWS_PER_STEP)]
        pending = [scores(r) for r in todo[:NA_LOOKAHEAD]]
        for t in range(NA_ROWS_PER_STEP):
            cur = pending.pop(0)
            if t + NA_LOOKAHEAD < NA_ROWS_PER_STEP:
                pending.append(scores(todo[t + NA_LOOKAHEAD]))
            finish(*cur)
        return carry

    lax.fori_loop(0, rows // NA_ROWS_PER_STEP, body, 0)


def _na_bias(rpb, rows):
    kh, kw = NA_KH, NA_KW
    depth = rpb.shape[0]
    col = np.arange(GRID_W)
    cs = np.clip(col - kw // 2, 0, GRID_W - kw)
    col_ok = (col[None, :] >= cs[:, None]) & (col[None, :] < cs[:, None] + kw)
    dc_idx = np.clip(col[None, :] - col[:, None], -(kw - 1), kw - 1) + (NA_KW - 1)
    by_col = jnp.take(rpb * LOG2E, jnp.asarray(dc_idx.reshape(-1)), axis=-1)
    by_col = by_col.reshape(depth, 4, 2 * NA_KH - 1, GRID_W, GRID_W)
    by_col = jnp.where(col_ok[None, None, None], by_col, NEG)
    pats = []
    pat_rows = list(range(kh // 2)) + [kh // 2] + list(range(rows - kh // 2 + 1, rows))
    for r in pat_rows:
        rs = min(max(r - kh // 2, 0), rows - kh)
        dr0 = rs - r + (NA_KH - 1)
        band = jnp.transpose(by_col[:, :, dr0:dr0 + kh], (0, 1, 3, 2, 4))
        pats.append(band.reshape(depth, 4 * GRID_W, kh * GRID_W))
    return jnp.stack(pats, axis=1)


def _neighborhood(proj, seq, layer, cache_k, cache_v, bias):
    t = proj.shape[0]
    nb = t // seq
    rows = seq // GRID_W
    n_ctx = cache_k.shape[2]
    l = layer
    npat = bias.shape[1]
    return pl.pallas_call(
        functools.partial(_na_kernel, rows),
        grid=(nb,),
        in_specs=[
            pl.BlockSpec((seq, 256), lambda b: (b, P_NAQ // 256)),
            pl.BlockSpec((seq, 256), lambda b: (b, P_NAK // 256)),
            pl.BlockSpec((seq, 256), lambda b: (b, P_NAV // 256)),
            pl.BlockSpec((None, None, n_ctx, 256), lambda b: (b, l, 0, 0)),
            pl.BlockSpec((None, None, n_ctx, 256), lambda b: (b, l, 0, 0)),
            _resident((None, npat, 4 * GRID_W, NA_KH * GRID_W), lambda b: (l, 0, 0, 0)),
        ],
        out_specs=pl.BlockSpec((seq, 256), lambda b: (b, 0)),
        out_shape=jax.ShapeDtypeStruct((t, 256), BF16),
        compiler_params=_params("arbitrary"),
        name="attn_neighborhood_latent",
    )(proj, proj, proj, cache_k, cache_v, bias)


def _mla_cache_kernel(ckv_ref, kr_ref, wkp_ref, wv_ref, gmk_ref, k_ref, v_ref):
    ckv_b = ckv_ref[...].astype(BF16)
    mk_pre = _mm(ckv_b, wkp_ref[...])
    krb = kr_ref[...]
    gmk = gmk_ref[...]
    for hd in range(4):
        lo = hd * MLA_HEAD_PAD
        kb = mk_pre[:, lo:lo + MLA_HEAD_PAD] + krb
        kb = kb * lax.rsqrt(jnp.sum(kb * kb, axis=-1, keepdims=True) * (1.0 / MLA_QK) + EPS)
        k_ref[:, lo:lo + MLA_HEAD_PAD] = (kb * gmk[:, lo:lo + MLA_HEAD_PAD]).astype(BF16)
    v_ref[...] = _mm(ckv_b, wv_ref[...]).T.astype(BF16)


def _mla_cache(cache_ckv, cache_kr_blk, wts):
    nb, depth, n_ctx, _ = cache_ckv.shape
    return pl.pallas_call(
        _mla_cache_kernel,
        grid=(depth, nb),
        in_specs=[
            pl.BlockSpec((None, None, n_ctx, 128), lambda l, b: (b, l, 0, 0)),
            pl.BlockSpec((None, None, n_ctx, 128), lambda l, b: (b, l, 0, 0)),
            pl.BlockSpec((None, 128, 512), lambda l, b: (l, 0, 0)),
            pl.BlockSpec((None, 128, 256), lambda l, b: (l, 0, 0)),
            pl.BlockSpec((None, 1, 512), lambda l, b: (l, 0, 0)),
        ],
        out_specs=[
            pl.BlockSpec((None, None, n_ctx, 512), lambda l, b: (b, l, 0, 0)),
            pl.BlockSpec((None, None, 256, n_ctx), lambda l, b: (b, l, 0, 0)),
        ],
        out_shape=[jax.ShapeDtypeStruct((nb, depth, n_ctx, 512), BF16),
                   jax.ShapeDtypeStruct((nb, depth, 256, n_ctx), BF16)],
        compiler_params=_params("arbitrary", "arbitrary"),
        name="mla_cache_keys",
    )(cache_ckv, cache_kr_blk, wts["w_kp"], wts["w_v"], wts["g_mk"])


def _merge_kernel(x_ref, mod_ref, g1_ref, o0_ref, o1_ref, o2_ref, o3_ref, wg_ref, wb_ref, wo_ref, y_ref):
    x = x_ref[...]
    mod = mod_ref[...]
    sh1, sc1, ga1 = mod[0:1], mod[1:2], mod[2:3]
    hb = (_rms_rows(x) * g1_ref[...] * (1.0 + sc1) + sh1).astype(BF16)
    y = None
    for m, o_ref in enumerate((o0_ref, o1_ref, o2_ref, o3_ref)):
        gate = jax.nn.sigmoid(_mm(hb, wg_ref[:, m * D_MODEL:(m + 1) * D_MODEL]))
        term = gate * _mm(o_ref[...], wb_ref[m])
        y = term if y is None else y + term
    y_ref[...] = x + ga1 * _mm(y.astype(BF16), wo_ref[...])


def _merge(x, mods, outs, wts, layer, *, seq, cond_row0, per_batch, tm):
    t = x.shape[0]
    tiles_per_seq = seq // tm
    l = layer

    def tok(i):
        return (i, 0)

    def modmap(i):
        return (l, (cond_row0 + i // tiles_per_seq) if per_batch else cond_row0, 0, 0)

    return pl.pallas_call(
        _merge_kernel,
        grid=(t // tm,),
        in_specs=[
            pl.BlockSpec((tm, D_MODEL), tok),
            pl.BlockSpec((None, None, N_MOD, D_MODEL), modmap),
            pl.BlockSpec((None, 1, D_MODEL), lambda i: (l, 0, 0)),
        ] + [pl.BlockSpec((tm, 256), tok)] * 4 + [
            _resident((None, D_MODEL, 4 * D_MODEL), lambda i: (l, 0, 0)),
            _resident((None, 4, 256, D_MODEL), lambda i: (l, 0, 0, 0)),
            _resident((None, D_MODEL, D_MODEL), lambda i: (l, 0, 0)),
        ],
        out_specs=pl.BlockSpec((tm, D_MODEL), tok),
        out_shape=jax.ShapeDtypeStruct((t, D_MODEL), F32),
        compiler_params=_params("arbitrary"),
        name="merge",
    )(x, mods, wts["norm1_g"], *outs, wts["w_g"], wts["w_b"], wts["w_o"])


def _split_bf16(v):
    hi = v.astype(BF16)
    lo = (v - hi.astype(F32)).astype(BF16)
    return hi, lo


def _moe_kernel(x_ref, mod_ref, g2_ref, wr_ref, w1_ref, w3_ref, w2_ref, ex_ref, y_ref):
    x = x_ref[...]
    mod = mod_ref[...]
    sh2, sc2, ga2 = mod[3:4], mod[4:5], mod[5:6]
    h = _rms_rows(x) * g2_ref[...] * (1.0 + sc2) + sh2
    hb, hl = _split_bf16(h)
    lg = _mm(hb, wr_ref[...])
    logits = lg[:, :LANE] + (lg[:, LANE:] + _mm(hl, wr_ref[:, :LANE]))
    lane = _lane_iota(logits.shape).astype(F32)
    big = jnp.float32(1e9)

    def first_argmax(v, vmax):
        return jnp.min(jnp.where(v == vmax, lane, big), axis=-1, keepdims=True)

    gl = jnp.where(lane < MOE_GROUPS, logits, NEG)
    gmax = jnp.max(gl, axis=-1, keepdims=True)
    g_top = 1.0 / jnp.sum(jnp.exp(gl - gmax), axis=-1, keepdims=True)
    g_idx = first_argmax(gl, gmax)
    e_lo = EXPERT_LANE0 + MOE_EPG * g_idx
    el = jnp.where((lane >= e_lo) & (lane < e_lo + MOE_EPG), logits, NEG)
    e1 = jnp.max(el, axis=-1, keepdims=True)
    i1 = first_argmax(el, e1)
    el2 = jnp.where(lane == i1, NEG, el)
    e2 = jnp.max(el2, axis=-1, keepdims=True)
    i2 = first_argmax(el2, e2)
    r = jnp.exp(e2 - e1)
    w_1 = 1.0 / (1.0 + r)
    gate = jnp.where(lane == i1, g_top * w_1, jnp.where(lane == i2, g_top * (r * w_1), 0.0))
    gate_hl = jnp.concatenate(_split_bf16(gate), axis=1)

    acc = None
    for c in range(MOE_CHUNKS):
        gate_c = _mm(gate_hl, ex_ref[c])
        cols = slice(c * MOE_CHUNK_W, (c + 1) * MOE_CHUNK_W)
        a = _mm(hb, w1_ref[:, cols])
        hid = (a * jax.nn.sigmoid(a)) * _mm(hb, w3_ref[:, cols]) * gate_c
        d = _mm(hid.astype(BF16), w2_ref[c])
        acc = d if acc is None else acc + d
    y_ref[...] = x + ga2 * acc


def _moe(x, mods, wts, consts, layer, *, seq, cond_row0, per_batch, tm):
    t = x.shape[0]
    tiles_per_seq = seq // tm
    l = layer

    def tok(i):
        return (i, 0)

    def modmap(i):
        return (l, (cond_row0 + i // tiles_per_seq) if per_batch else cond_row0, 0, 0)

    return pl.pallas_call(
        _moe_kernel,
        grid=(t // tm,),
        in_specs=[
            pl.BlockSpec((tm, D_MODEL), tok),
            pl.BlockSpec((None, None, N_MOD, D_MODEL), modmap),
            pl.BlockSpec((None, 1, D_MODEL), lambda i: (l, 0, 0)),
            _resident((None, D_MODEL, 2 * LANE), lambda i: (l, 0, 0)),
            _resident((None, D_MODEL, MOE_EXPERTS * MOE_FF), lambda i: (l, 0, 0)),
            _resident((None, D_MODEL, MOE_EXPERTS * MOE_FF), lambda i: (l, 0, 0)),
            _resident((None, MOE_CHUNKS, MOE_CHUNK_W, D_MODEL), lambda i: (l, 0, 0, 0)),
            _resident((MOE_CHUNKS, 2 * LANE, MOE_CHUNK_W), lambda i: (0, 0, 0)),
        ],
        out_specs=pl.BlockSpec((tm, D_MODEL), tok),
        out_shape=jax.ShapeDtypeStruct((t, D_MODEL), F32),
        compiler_params=_params("arbitrary"),
        name="moe",
    )(x, mods, wts["norm2_g"], wts["w_r"], wts["w_1"], wts["w_3"], wts["w_2"], consts["expand"])


def _block_ones(n, seg):
    i = np.arange(n)
    return (i[:, None] // seg == i[None, :] // seg).astype(np.float32)


def _rope_perm(n, lo, hi, quarter):
    p = np.zeros((n, n), np.float32)
    for i in range(lo, hi):
        if (i - lo) % (2 * quarter) < quarter:
            p[i + quarter, i] = -1.0
        else:
            p[i - quarter, i] = 1.0
    return p


def _axial_tables(seq, rot_dim):
    tpos = jnp.arange(seq)
    row = (tpos // GRID_W).astype(F32)
    col = (tpos % GRID_W).astype(F32)
    half = rot_dim // 2
    freqs = ROPE_BASE ** (-jnp.arange(0, half, 2, dtype=F32) / half)
    ar, ac = row[:, None] * freqs, col[:, None] * freqs
    ang = jnp.concatenate([ar, ar, ac, ac], axis=-1)
    return jnp.cos(ang), jnp.sin(ang)


def _constants(seq):
    cd, sd = _axial_tables(seq, DIFF_DH)
    cg, sg = _axial_tables(seq, HEAD_DIM)
    cm32, sm32 = _axial_tables(seq, MLA_ROPE)
    ones = jnp.ones((seq, MLA_NOPE), F32)
    zeros = jnp.zeros((seq, MLA_NOPE), F32)
    pad1 = jnp.ones((seq, MLA_HEAD_PAD - MLA_QK), F32)
    pad0 = jnp.zeros((seq, MLA_HEAD_PAD - MLA_QK), F32)
    expand = np.zeros((MOE_CHUNKS, 2 * LANE, MOE_CHUNK_W), np.float32)
    for e in range(MOE_EXPERTS):
        c, j = divmod(e, MOE_EXPERTS // MOE_CHUNKS)
        expand[c, EXPERT_LANE0 + e, j * MOE_FF:(j + 1) * MOE_FF] = 1.0
        expand[c, LANE + EXPERT_LANE0 + e, j * MOE_FF:(j + 1) * MOE_FF] = 1.0
    return dict(
        s64=jnp.asarray(_block_ones(256, 64), BF16),
        s32=jnp.asarray(_block_ones(256, 32), BF16),
        p32=jnp.asarray(_rope_perm(256, 0, 256, DIFF_DH // 4), BF16),
        p64=jnp.asarray(_rope_perm(256, 0, 256, HEAD_DIM // 4), BF16),
        pm=jnp.asarray(_rope_perm(128, MLA_NOPE, MLA_QK, MLA_ROPE // 4), BF16),
        cos_d=jnp.tile(cd, (1, 256 // DIFF_DH)), sin_d=jnp.tile(sd, (1, 256 // DIFF_DH)),
        cos_g=jnp.tile(cg, (1, 256 // HEAD_DIM)), sin_g=jnp.tile(sg, (1, 256 // HEAD_DIM)),
        cos_m=jnp.concatenate([ones, cm32, pad1], axis=-1),
        sin_m=jnp.concatenate([zeros, sm32, pad0], axis=-1),
        expand=jnp.asarray(expand, BF16),
    )


def _layout_weights(p):
    depth = p["w_in"].shape[0]
    sizes = (256, 256, 256, 256, 256, 256, 256, 128, 32, 256, 128, 128, 4 * D_MODEL)
    cuts = np.concatenate([[0], np.cumsum(sizes)])
    w_in = p["w_in"]

    def seg(i):
        return w_in[:, :, cuts[i]:cuts[i + 1]]

    perm = jnp.array([0, 2, 1, 3])
    gq_w = seg(9).reshape(depth, D_MODEL, 4, HEAD_DIM)[:, :, perm].reshape(depth, D_MODEL, 256)
    zpad = lambda n: jnp.zeros((depth, D_MODEL, n), F32)
    kr_blk = jnp.concatenate([zpad(MLA_NOPE), seg(8), zpad(MLA_HEAD_PAD - MLA_QK)], axis=-1)
    w_a = jnp.concatenate([seg(0), seg(1), seg(2), seg(3), seg(4), seg(5), seg(6), seg(7), kr_blk,
                           gq_w, seg(10), seg(11)], axis=-1).astype(BF16)
    w_g = seg(12).astype(BF16)

    ones = lambda n: jnp.ones((depth, n), F32)
    tile = lambda g, n: jnp.tile(g, (1, n))
    g_a = jnp.concatenate([
        tile(p["na_q_g"], 4) * (HEAD_DIM ** -0.5 * LOG2E), tile(p["na_k_g"], 4), ones(256),
        tile(p["diff_q_g"], 8) * (DIFF_DH ** -0.5 * LOG2E), tile(p["diff_k_g"], 8), ones(256),
        p["mla_q_a_g"], p["mla_kv_a_g"], ones(128),
        tile(p["gqa_q_g"], 4) * (HEAD_DIM ** -0.5 * LOG2E), tile(p["gqa_k_g"], 2), ones(128)], axis=-1)[:, None, :]

    def pad_heads(g):
        return jnp.tile(jnp.pad(g, ((0, 0), (0, MLA_HEAD_PAD - MLA_QK))), (1, 4))[:, None, :]

    w_qb = jnp.pad(p["mla_w_q_b"].reshape(depth, 256, 4, MLA_QK),
                   ((0, 0), (0, 0), (0, 0), (0, MLA_HEAD_PAD - MLA_QK))).reshape(depth, 256, 512).astype(BF16)
    kvb = p["mla_w_kv_b"].reshape(depth, 128, 4, 128)
    w_kp = jnp.pad(kvb[..., :MLA_NOPE], ((0, 0), (0, 0), (0, 0), (0, MLA_HEAD_PAD - MLA_NOPE)))
    w_kp = w_kp.reshape(depth, 128, 512).astype(BF16)
    w_v = kvb[..., MLA_NOPE:].reshape(depth, 128, 256).astype(BF16)

    w_b = p["w_branch"]
    w_b3 = w_b[:, 3].reshape(depth, 4, HEAD_DIM, D_MODEL)[:, perm].reshape(depth, 256, D_MODEL)
    w_b = jnp.concatenate([w_b[:, :3], w_b3[:, None]], axis=1).astype(BF16)

    w_r = jnp.zeros((depth, D_MODEL, LANE), F32)
    w_r = w_r.at[:, :, :MOE_GROUPS].set(p["moe_w_group"])
    w_r = w_r.at[:, :, EXPERT_LANE0:EXPERT_LANE0 + MOE_EXPERTS].set(p["moe_w_expert"])
    w_r_hi = w_r.astype(BF16)
    w_r_lo = (w_r - w_r_hi.astype(F32)).astype(BF16)
    w_r = jnp.concatenate([w_r_hi, w_r_lo], axis=-1)

    return dict(
        norm1_g=p["norm1_g"][:, None, :], norm2_g=p["norm2_g"][:, None, :],
        w_a=w_a, g_a=g_a, w_g=w_g, w_qb=w_qb, w_kp=w_kp, w_v=w_v,
        g_mq=pad_heads(p["mla_q_g"]) * (MLA_QK ** -0.5 * LOG2E), g_mk=pad_heads(p["mla_k_g"]),
        w_b=w_b, w_o=p["w_out"].astype(BF16),
        w_r=w_r,
        w_1=p["moe_w1"].astype(BF16), w_3=p["moe_w3"].astype(BF16),
        w_2=p["moe_w2"].reshape(depth, MOE_CHUNKS, MOE_CHUNK_W, D_MODEL).astype(BF16),
    )


def _pick_tile(n, pref):
    t = min(n, pref)
    while n % t:
        t //= 2
    return t


def kernel(x_prompt, x_sample, cache_na_k, cache_na_v, cache_diff_k, cache_diff_v, cache_mla_ckv, cache_mla_krope, cache_gqa_k, cache_gqa_v, c, c_ctx, w_mod, b_mod, norm1_g, norm2_g, w_in, na_q_g, na_k_g, na_rpb, diff_q_g, diff_k_g, diff_lq1, diff_lk1, diff_lq2, diff_lk2, diff_sub_g, mla_q_a_g, mla_w_q_b, mla_kv_a_g, mla_w_kv_b, mla_q_g, mla_k_g, gqa_q_g, gqa_k_g, w_branch, w_out, moe_w_group, moe_w_expert, moe_w1, moe_w3, moe_w2):
    nbp, seq_p, _ = x_prompt.shape
    nbs, seq_s, _ = x_sample.shape
    depth = w_in.shape[0]
    n_ctx = cache_na_k.shape[2]
    rows = seq_s // GRID_W

    wts = _layout_weights(dict(
        w_in=w_in, na_q_g=na_q_g, na_k_g=na_k_g, diff_q_g=diff_q_g, diff_k_g=diff_k_g,
        mla_q_a_g=mla_q_a_g, mla_kv_a_g=mla_kv_a_g, gqa_q_g=gqa_q_g, gqa_k_g=gqa_k_g,
        mla_w_q_b=mla_w_q_b, mla_w_kv_b=mla_w_kv_b, mla_q_g=mla_q_g, mla_k_g=mla_k_g,
        w_branch=w_branch, w_out=w_out, moe_w_group=moe_w_group, moe_w_expert=moe_w_expert,
        moe_w1=moe_w1, moe_w3=moe_w3, moe_w2=moe_w2, norm1_g=norm1_g, norm2_g=norm2_g))
    consts = _constants(seq_s)

    n_cond = nbs + 1
    cond = jnp.concatenate([c, c_ctx[None, :], jnp.zeros((-n_cond % 8, D_MODEL), F32)], axis=0)
    mods = _modulation(cond, w_mod, b_mod).reshape(depth, cond.shape[0], N_MOD, D_MODEL)

    flat = lambda a: a.reshape(a.shape[0], a.shape[1], a.shape[2], -1).astype(BF16)
    flat_t = lambda a: jnp.swapaxes(flat(a), 2, 3)
    c_na_k, c_na_v = flat(cache_na_k), flat(cache_na_v)
    c_df_k, c_df_vt = flat(cache_diff_k), flat_t(cache_diff_v)
    c_g_k, c_g_vt = flat(cache_gqa_k), flat_t(cache_gqa_v)
    kr_blk = jnp.pad(cache_mla_krope, ((0, 0), (0, 0), (0, 0), (MLA_NOPE, MLA_HEAD_PAD - MLA_QK)))
    c_m_k, c_m_vt = _mla_cache(cache_mla_ckv, kr_blk, wts)

    gsub = jnp.tile(diff_sub_g, (1, 4))[:, None, :]
    lam_args = [a[:, None, :] for a in (diff_lq1, diff_lk1, diff_lq2, diff_lk2)]

    xp = x_prompt.reshape(nbp * seq_p, D_MODEL)
    xs = x_sample.reshape(nbs * seq_s, D_MODEL)
    tm_p = _pick_tile(seq_p, 512)
    tm_s = _pick_tile(seq_s, 512)
    states = []
    bias = _na_bias(na_rpb, rows)
    for l in range(depth):
        lam_init = 0.8 - 0.6 * math.exp(-0.3 * l)
        diff_args = lam_args + [gsub, consts["s64"]]

        proj, vt, state = _project(xp, mods, wts, consts, l, rope=False, seq=seq_p, cond_row0=nbs, tm=tm_p)
        outs = [
            _attend("heads64", proj, vt, seq_p, l, tq=tm_p),
            _attend("diff", proj, vt, seq_p, l, diff_args=diff_args, lam_init=lam_init, tq=tm_p),
            _attend("mla", proj, vt, seq_p, l, tq=tm_p),
            _attend("gqa", proj, vt, seq_p, l, tq=tm_p),
        ]
        xp = _merge(xp, mods, outs, wts, l, seq=seq_p, cond_row0=nbs, per_batch=False, tm=tm_p)
        xp = _moe(xp, mods, wts, consts, l, seq=seq_p, cond_row0=nbs, per_batch=False, tm=tm_p)
        states.append(state.reshape(nbp, seq_p, S_TOTAL))

        proj, vt = _project(xs, mods, wts, consts, l, rope=True, seq=seq_s, cond_row0=0, tm=tm_s)
        outs = [
            _neighborhood(proj, seq_s, l, c_na_k, c_na_v, bias),
            _attend("diff", proj, vt, seq_s, l, cache_k=c_df_k, cache_vt=c_df_vt, diff_args=diff_args,
                    lam_init=lam_init),
            _attend("mla", proj, vt, seq_s, l, cache_k=c_m_k, cache_vt=c_m_vt),
            _attend("gqa", proj, vt, seq_s, l, cache_k=c_g_k, cache_vt=c_g_vt),
        ]
        xs = _merge(xs, mods, outs, wts, l, seq=seq_s, cond_row0=0, per_batch=True, tm=tm_s)
        xs = _moe(xs, mods, wts, consts, l, seq=seq_s, cond_row0=0, per_batch=True, tm=tm_s)

    st = jnp.stack(states, axis=1)
    heads = lambda a, h: a.reshape(nbp, depth, seq_p, h, HEAD_DIM)
    return (
        xp.reshape(nbp, seq_p, D_MODEL),
        xs.reshape(nbs, seq_s, D_MODEL),
        heads(st[..., 0:256], 4), heads(st[..., 256:512], 4),
        heads(st[..., 512:768], 4), heads(st[..., 768:1024], 4),
        st[..., 1024:1152], st[..., 1152 + MLA_NOPE:1152 + MLA_QK],
        heads(st[..., 1280:1408], 2), heads(st[..., 1408:1536], 2),
    )
```

```python
import functools
import math

import numpy as np
import jax
import jax.numpy as jnp
from jax import lax
from jax.experimental import pallas as pl
from jax.experimental.pallas import tpu as pltpu

F32 = jnp.float32
BF16 = jnp.bfloat16

D_MODEL = 1024
GRID_W = 64
HEAD_DIM = 64
NA_KH = 8
NA_KW = 16
DIFF_DH = 32
MLA_HEAD_PAD = 128
MLA_QK = 96
MLA_NOPE = 64
MLA_ROPE = 32
MOE_GROUPS = 4
MOE_EPG = 8
MOE_EXPERTS = 32
MOE_FF = 128
N_MOD = 6
ROPE_BASE = 10000.0
EPS = 1e-6
NEG = -1e30
LANE = 128
EXPERT_LANE0 = 32
MOE_CHUNKS = 8
MOE_CHUNK_W = MOE_EXPERTS * MOE_FF // MOE_CHUNKS

P_NAQ, P_NAK, P_NAV = 0, 256, 512
P_DFQ, P_DFK, P_GQ = 768, 1024, 1280
P_MQ, P_MK, P_GK = 1536, 2048, 2560
P_TOTAL = 2688
VT_ROWS_LATENT = 640
VT_ROWS_CONTEXT = 896
LOG2E = math.log2(math.e)
S_TOTAL = 1536

Z_TOTAL = 2560
VMEM_LIMIT = 56 * 1024 * 1024


def _mm(a, b):
    return jnp.dot(a, b, preferred_element_type=F32)


def _nt(a, b):
    return lax.dot_general(a, b, (((1,), (1,)), ((), ())), preferred_element_type=F32)


def _resident(shape, index_map):
    return pl.BlockSpec(shape, index_map, pipeline_mode=pl.Buffered(1))


def _params(*sem):
    return pltpu.CompilerParams(dimension_semantics=sem, vmem_limit_bytes=VMEM_LIMIT)


def _lane_iota(shape):
    return lax.broadcasted_iota(jnp.int32, shape, len(shape) - 1)


def _rms_rows(x):
    return x * lax.rsqrt(jnp.mean(x * x, axis=-1, keepdims=True) + EPS)


def _seg_norm(v, ones_bd, n):
    ms = _mm((v * v).astype(BF16), ones_bd) * (1.0 / n)
    return v * lax.rsqrt(ms + EPS)


def _mod_kernel(c_ref, w_ref, b_ref, o_ref):
    c = c_ref[...]
    s = c * jax.nn.sigmoid(c)
    o_ref[...] = _mm(s.astype(BF16), w_ref[...].astype(BF16)) + b_ref[...]


def _modulation(cond, w_mod, b_mod):
    depth = w_mod.shape[0]
    rows = cond.shape[0]
    nblk = N_MOD
    return pl.pallas_call(
        _mod_kernel,
        grid=(depth, nblk),
        in_specs=[
            pl.BlockSpec((rows, D_MODEL), lambda l, j: (0, 0)),
            pl.BlockSpec((None, D_MODEL, D_MODEL), lambda l, j: (l, 0, j)),
            pl.BlockSpec((None, 1, D_MODEL), lambda l, j: (l, 0, j)),
        ],
        out_specs=pl.BlockSpec((None, rows, D_MODEL), lambda l, j: (l, 0, j)),
        out_shape=jax.ShapeDtypeStruct((depth, rows, N_MOD * D_MODEL), F32),
        compiler_params=_params("arbitrary", "arbitrary"),
        name="modulation",
    )(cond, w_mod, b_mod.reshape(depth, 1, N_MOD * D_MODEL))


def _proj_kernel(rope, *refs):
    it = iter(refs)
    x_ref, mod_ref, g1_ref, wa_ref, ga_ref = (next(it) for _ in range(5))
    wqb_ref, gmq_ref, wkp_ref, wv_ref, gmk_ref = (next(it) for _ in range(5))
    s64_ref, s32_ref = next(it), next(it)
    if rope:
        p32_ref, p64_ref, pm_ref = next(it), next(it), next(it)
        cd_ref, sd_ref, cg_ref, sg_ref, cm_ref, sm_ref = (next(it) for _ in range(6))
    proj_ref = next(it)
    vt_ref = next(it)
    state_ref = None if rope else next(it)

    x = x_ref[...]
    mod = mod_ref[...]
    sh1, sc1 = mod[0:1], mod[1:2]
    h = _rms_rows(x) * g1_ref[...] * (1.0 + sc1) + sh1
    z = _mm(h.astype(BF16), wa_ref[...])
    ga = ga_ref[...]
    s64 = s64_ref[...]
    s32 = s32_ref[...]

    def seg(lo, w):
        return z[:, lo:lo + w]

    def gain(lo, w):
        return ga[:, lo:lo + w]

    def rot(v, perm, cos, sin):
        return v * cos + _mm(v.astype(BF16), perm) * sin

    naq = _seg_norm(seg(0, 256), s64, 64) * gain(0, 256)
    nak = _seg_norm(seg(256, 256), s64, 64) * gain(256, 256)
    nav = seg(512, 256)
    dfq = _seg_norm(seg(768, 256), s32, 32) * gain(768, 256)
    dfk = _seg_norm(seg(1024, 256), s32, 32) * gain(1024, 256)
    dfv = seg(1280, 256)
    cq = _rms_rows(seg(1536, 256)) * gain(1536, 256)
    ckv = _rms_rows(seg(1792, 128)) * gain(1792, 128)
    krb = seg(1920, 128)
    mq_pre = _mm(cq.astype(BF16), wqb_ref[...])
    ckv_b = ckv.astype(BF16)
    mk_pre = _mm(ckv_b, wkp_ref[...])
    mv = _mm(ckv_b, wv_ref[...])
    gq = _seg_norm(seg(2048, 256), s64, 64) * gain(2048, 256)
    gk = _seg_norm(seg(2304, 128), s64[:128, :128], 64) * gain(2304, 128)
    gv = seg(2432, 128)

    if not rope:
        state_ref[:, 0:256] = nak
        state_ref[:, 256:512] = nav
        state_ref[:, 512:768] = dfk
        state_ref[:, 768:1024] = dfv
        state_ref[:, 1024:1152] = ckv
        state_ref[:, 1152:1280] = krb
        state_ref[:, 1280:1408] = gk
        state_ref[:, 1408:1536] = gv
    else:
        p32, p64 = p32_ref[...], p64_ref[...]
        cd, sd, cg, sg = cd_ref[...], sd_ref[...], cg_ref[...], sg_ref[...]
        dfq = rot(dfq, p32, cd, sd)
        dfk = rot(dfk, p32, cd, sd)
        gq = rot(gq, p64, cg, sg)
        gk = rot(gk, p64[:128, :128], cg[:, :128], sg[:, :128])

    proj_ref[:, P_NAQ:P_NAQ + 256] = naq.astype(BF16)
    proj_ref[:, P_NAK:P_NAK + 256] = nak.astype(BF16)
    proj_ref[:, P_NAV:P_NAV + 256] = nav.astype(BF16)
    proj_ref[:, P_DFQ:P_DFQ + 256] = dfq.astype(BF16)
    proj_ref[:, P_DFK:P_DFK + 256] = dfk.astype(BF16)
    gmq = gmq_ref[...]
    gmk = gmk_ref[...]
    for hd in range(4):
        lo = hd * MLA_HEAD_PAD
        qb = mq_pre[:, lo:lo + MLA_HEAD_PAD]
        qb = qb * lax.rsqrt(jnp.sum(qb * qb, axis=-1, keepdims=True) * (1.0 / MLA_QK) + EPS)
        qb = qb * gmq[:, lo:lo + MLA_HEAD_PAD]
        kb = mk_pre[:, lo:lo + MLA_HEAD_PAD] + krb
        kb = kb * lax.rsqrt(jnp.sum(kb * kb, axis=-1, keepdims=True) * (1.0 / MLA_QK) + EPS)
        kb = kb * gmk[:, lo:lo + MLA_HEAD_PAD]
        if rope:
            pm, cm, sm = pm_ref[...], cm_ref[...], sm_ref[...]
            qb = rot(qb, pm, cm, sm)
            kb = rot(kb, pm, cm, sm)
        proj_ref[:, P_MQ + lo:P_MQ + lo + MLA_HEAD_PAD] = qb.astype(BF16)
        proj_ref[:, P_MK + lo:P_MK + lo + MLA_HEAD_PAD] = kb.astype(BF16)
    proj_ref[:, P_GQ:P_GQ + 256] = gq.astype(BF16)
    proj_ref[:, P_GK:P_GK + 128] = gk.astype(BF16)
    base = 0
    if not rope:
        vt_ref[0:256, :] = nav.T.astype(BF16)
        base = 256
    vt_ref[base:base + 256, :] = dfv.T.astype(BF16)
    vt_ref[base + 256:base + 512, :] = mv.T.astype(BF16)
    vt_ref[base + 512:base + 640, :] = gv.T.astype(BF16)


def _project(x, mods, wts, consts, layer, *, rope, seq, cond_row0, tm):
    t = x.shape[0]
    tiles_per_seq = seq // tm
    nb = t // seq
    l = layer

    def tok(i, b):
        return (b * tiles_per_seq + i, 0)

    def cst(i, b):
        return (0, 0)

    def lay(i, b):
        return (l, 0, 0)

    def modmap(i, b):
        return (l, (cond_row0 + b) if rope else cond_row0, 0, 0)

    in_specs = [
        pl.BlockSpec((tm, D_MODEL), tok),
        pl.BlockSpec((None, None, N_MOD, D_MODEL), modmap),
        pl.BlockSpec((None, 1, D_MODEL), lay),
        _resident((None, D_MODEL, Z_TOTAL), lay),
        pl.BlockSpec((None, 1, Z_TOTAL), lay),
        _resident((None, 256, 512), lay),
        pl.BlockSpec((None, 1, 512), lay),
        _resident((None, 128, 512), lay),
        _resident((None, 128, 256), lay),
        pl.BlockSpec((None, 1, 512), lay),
        _resident((256, 256), cst),
        _resident((256, 256), cst),
    ]
    args = [x, mods, wts["norm1_g"], wts["w_a"], wts["g_a"], wts["w_qb"], wts["g_mq"],
            wts["w_kp"], wts["w_v"], wts["g_mk"], consts["s64"], consts["s32"]]
    if rope:
        def pos(i, b):
            return (i, 0)
        in_specs += [_resident((256, 256), cst), _resident((256, 256), cst), _resident((128, 128), cst)]
        in_specs += [pl.BlockSpec((tm, 256), pos)] * 4 + [pl.BlockSpec((tm, 128), pos)] * 2
        args += [consts["p32"], consts["p64"], consts["pm"], consts["cos_d"], consts["sin_d"],
                 consts["cos_g"], consts["sin_g"], consts["cos_m"], consts["sin_m"]]
    vt_rows = VT_ROWS_LATENT if rope else VT_ROWS_CONTEXT
    out_specs = [pl.BlockSpec((tm, P_TOTAL), tok),
                 pl.BlockSpec((None, None, vt_rows, tm), lambda i, b: (b, i, 0, 0))]
    out_shape = [jax.ShapeDtypeStruct((t, P_TOTAL), BF16),
                 jax.ShapeDtypeStruct((nb, tiles_per_seq, vt_rows, tm), BF16)]
    if not rope:
        out_specs.append(pl.BlockSpec((tm, S_TOTAL), tok))
        out_shape.append(jax.ShapeDtypeStruct((t, S_TOTAL), F32))
    return pl.pallas_call(
        functools.partial(_proj_kernel, rope),
        grid=(tiles_per_seq, nb),
        in_specs=in_specs,
        out_specs=out_specs,
        out_shape=out_shape,
        compiler_params=_params("arbitrary", "arbitrary"),
        name="project_latent" if rope else "project_context",
    )(*args)


_BRANCHES = {
    "heads64": [(0, 256, 64 * h, 64 * h + 64, 0, 256, 64 * h) for h in range(4)],
    "diff": [(0, 256, 32 * (2 * h + i), 32 * (2 * h + i) + 32, 0, 256, 64 * h) for h in range(4) for i in range(2)],
    "mla": [(128 * h, 128, None, None, 128 * h, 128, 64 * h) for h in range(4)],
    "gqa": [(128 * g, 128, 64 * j, 64 * j + 64, 0, 128, 64 * j) for g in range(2) for j in range(2)],
}
ONES_ROWS = 16
ACC_ROWS = HEAD_DIM + ONES_ROWS
SCORE_LOOKAHEAD = 3
CHUNK_TILES = 8
LAZY_MAX_HEADROOM = 64.0


def _flash_kernel(kind, n_chunks, has_ctx, lam_init, *refs):
    maps = _BRANCHES[kind]
    it = iter(refs)
    q_ref, k_ref, vt_ref = next(it), next(it), next(it)
    if has_ctx:
        kctx_ref, vtctx_ref = next(it), next(it)
    if kind == "diff":
        lq1_ref, lk1_ref, lq2_ref, lk2_ref, gsub_ref, s64_ref = (next(it) for _ in range(6))
    o_ref = next(it)
    tq = q_ref.shape[0]
    kc = vt_ref.shape[-1]

    q = q_ref[...]
    qm = []
    for (qlo, qw, mlo, mhi, _, _, _) in maps:
        qj = q[:, qlo:qlo + qw]
        if mlo is not None:
            lane = _lane_iota(qj.shape)
            qj = jnp.where((lane >= mlo) & (lane < mhi), qj, jnp.zeros_like(qj))
        qm.append(qj)

    def chunk(kch, vtch, state, lazy):
        per_map, risk = state
        ones = jnp.ones((ONES_ROWS, kch.shape[0]), BF16)

        def scores(j):
            klo, kw = maps[j][4], maps[j][5]
            return _nt(kch[:, klo:klo + kw], qm[j])

        new_maps = []
        pending = [scores(j) for j in range(min(SCORE_LOOKAHEAD, len(maps)))]
        for j, ((m_prev, acc), (_, _, _, _, _, _, vrow)) in enumerate(zip(per_map, maps)):
            st = pending.pop(0)
            if j + SCORE_LOOKAHEAD < len(maps):
                pending.append(scores(j + SCORE_LOOKAHEAD))
            vaug = jnp.concatenate([vtch[vrow:vrow + HEAD_DIM, :], ones], axis=0)
            cmax = jnp.max(st, axis=0, keepdims=True)
            m_new = jnp.maximum(m_prev, cmax)
            alpha = jnp.exp2(m_prev - m_new)
            if lazy:
                pt = jnp.exp2(st - m_prev).astype(BF16)
                acc = (acc + _mm(vaug, pt)) * alpha
                risk = jnp.maximum(risk, cmax - m_prev)
            else:
                pt = jnp.exp2(st - m_new).astype(BF16)
                acc = acc * alpha + _mm(vaug, pt)
            new_maps.append((m_new, acc))
        return tuple(new_maps), risk

    def finish(state):
        heads = [acc[0:HEAD_DIM] * (1.0 / acc[HEAD_DIM:HEAD_DIM + 1]) for _, acc in state[0]]
        if kind == "diff":
            lam = (jnp.exp(jnp.sum(lq1_ref[...] * lk1_ref[...], axis=-1, keepdims=True))
                   - jnp.exp(jnp.sum(lq2_ref[...] * lk2_ref[...], axis=-1, keepdims=True)) + lam_init)
            heads = [heads[2 * h] - lam * heads[2 * h + 1] for h in range(4)]
        o = jnp.concatenate(heads, axis=0).T
        if kind == "diff":
            o = _seg_norm(o, s64_ref[...], 64) * gsub_ref[...] * (1.0 - lam_init)
        o_ref[...] = o.astype(BF16)

    def sweep(state, lazy):
        g = math.gcd(CHUNK_TILES, n_chunks)
        span = g * kc

        def body(c, st):
            vtch = jnp.concatenate([vt_ref[c * g + i] for i in range(g)], axis=1)
            return chunk(k_ref[pl.ds(pl.multiple_of(c * span, span), span), :], vtch, st, lazy)
        return lax.fori_loop(0, n_chunks // g, body, state)

    state = (tuple((jnp.full((1, tq), NEG, F32), jnp.zeros((ACC_ROWS, tq), F32)) for _ in maps),
             jnp.zeros((1, tq), F32))
    if not has_ctx:
        assert n_chunks == 1
        finish(chunk(k_ref[...], vt_ref[0], state, False))
    else:
        state = chunk(kctx_ref[...], vtctx_ref[...], state, False)
        fast = sweep(state, True)
        unsafe = jnp.max(fast[1]) > LAZY_MAX_HEADROOM

        @pl.when(jnp.logical_not(unsafe))
        def _():
            finish(fast)

        @pl.when(unsafe)
        def _():
            finish(sweep(state, False))


_BRANCH_COLS = {
    "heads64": (P_NAQ, 256, P_NAK, 256, -256, 256),
    "diff": (P_DFQ, 256, P_DFK, 256, 0, 256),
    "mla": (P_MQ, 512, P_MK, 512, 256, 256),
    "gqa": (P_GQ, 256, P_GK, 128, 512, 128),
}


def _attend(kind, proj, vt, seq, layer, *, cache_k=None, cache_vt=None, diff_args=None, lam_init=0.0, tq=256):
    t = proj.shape[0]
    nb, n_chunks, vt_rows, kc = vt.shape
    nq = seq // tq
    qo, qw, ko, kw, vo, vw = _BRANCH_COLS[kind]
    vo += vt_rows - VT_ROWS_LATENT
    maps = _BRANCHES[kind]
    has_ctx = cache_k is not None
    l = layer

    in_specs = [
        pl.BlockSpec((tq, qw), lambda b, i: (b * nq + i, qo // qw)),
        pl.BlockSpec((seq, kw), lambda b, i: (b, ko // kw)),
        pl.BlockSpec((None, n_chunks, vw, kc), lambda b, i: (b, 0, vo // vw, 0)),
    ]
    args = [proj, proj, vt]
    if has_ctx:
        n_ctx = cache_k.shape[2]
        in_specs += [
            pl.BlockSpec((None, None, n_ctx, kw), lambda b, i: (b, l, 0, 0)),
            pl.BlockSpec((None, None, vw, n_ctx), lambda b, i: (b, l, 0, 0)),
        ]
        args += [cache_k, cache_vt]
    if kind == "diff":
        in_specs += [pl.BlockSpec((None, 1, DIFF_DH), lambda b, i: (l, 0, 0))] * 4
        in_specs += [pl.BlockSpec((None, 1, 256), lambda b, i: (l, 0, 0)),
                     pl.BlockSpec((256, 256), lambda b, i: (0, 0))]
        args += list(diff_args)
    return pl.pallas_call(
        functools.partial(_flash_kernel, kind, n_chunks, has_ctx, lam_init),
        grid=(nb, nq),
        in_specs=in_specs,
        out_specs=pl.BlockSpec((tq, 256), lambda b, i: (b * nq + i, 0)),
        out_shape=jax.ShapeDtypeStruct((t, 256), BF16),
        compiler_params=_params("arbitrary", "arbitrary"),
        name="attn_" + kind + ("_latent" if has_ctx else "_context"),
    )(*args)


NA_ROWS_PER_STEP = 4
NA_LOOKAHEAD = 2


def _na_kernel(rows, q_ref, k_ref, v_ref, kc_ref, vc_ref, bias_ref, o_ref):
    kh = NA_KH
    band = kh * GRID_W
    lane = _lane_iota((GRID_W, 256))
    head_masks = [(lane >= 64 * h) & (lane < 64 * h + 64) for h in range(4)]
    kctx = kc_ref[...]
    vctx = vc_ref[...]

    def scores(r):
        rs = jnp.clip(r - kh // 2, 0, rows - kh)
        pat = jnp.where(r < kh // 2, r, jnp.where(r > rows - kh // 2, r - (rows - kh), kh // 2))
        q = q_ref[pl.ds(pl.multiple_of(r * GRID_W, GRID_W), GRID_W), :]
        qs = jnp.concatenate([jnp.where(mk, q, jnp.zeros_like(q)) for mk in head_masks], axis=0)
        start = pl.multiple_of(rs * GRID_W, GRID_W)
        sb = _nt(qs, k_ref[pl.ds(start, band), :]) + bias_ref[pat]
        sc = _nt(qs, kctx)
        return r, start, sb, sc

    def finish(r, start, sb, sc):
        m = jnp.maximum(jnp.max(sb, axis=-1, keepdims=True), jnp.max(sc, axis=-1, keepdims=True))
        pb = jnp.exp2(sb - m)
        pc = jnp.exp2(sc - m)
        den = jnp.sum(pb, axis=-1, keepdims=True) + jnp.sum(pc, axis=-1, keepdims=True)
        vb = v_ref[pl.ds(start, band), :]
        of = (_mm(pb.astype(BF16), vb) + _mm(pc.astype(BF16), vctx)) * (1.0 / den)
        o = jnp.zeros((GRID_W, 256), F32)
        for h in range(4):
            o = jnp.where(head_masks[h], of[GRID_W * h:GRID_W * (h + 1), :], o)
        o_ref[pl.ds(pl.multiple_of(r * GRID_W, GRID_W), GRID_W), :] = o.astype(BF16)

    def body(i, carry):
        todo = [i * NA_ROWS_PER_STEP + t for t in range(NA_ROWS_PER_STEP)]
        pending = [scores(r) for r in todo[:NA_LOOKAHEAD]]
        for t in range(NA_ROWS_PER_STEP):
            cur = pending.pop(0)
            if t + NA_LOOKAHEAD < NA_ROWS_PER_STEP:
                pending.append(scores(todo[t + NA_LOOKAHEAD]))
            finish(*cur)
        return carry

    lax.fori_loop(0, rows // NA_ROWS_PER_STEP, body, 0)


def _na_bias(rpb, rows):
    kh, kw = NA_KH, NA_KW
    depth = rpb.shape[0]
    col = np.arange(GRID_W)
    cs = np.clip(col - kw // 2, 0, GRID_W - kw)
    col_ok = (col[None, :] >= cs[:, None]) & (col[None, :] < cs[:, None] + kw)
    dc_idx = np.clip(col[None, :] - col[:, None], -(kw - 1), kw - 1) + (NA_KW - 1)
    by_col = jnp.take(rpb * LOG2E, jnp.asarray(dc_idx.reshape(-1)), axis=-1)
    by_col = by_col.reshape(depth, 4, 2 * NA_KH - 1, GRID_W, GRID_W)
    by_col = jnp.where(col_ok[None, None, None], by_col, NEG)
    pats = []
    pat_rows = list(range(kh // 2)) + [kh // 2] + list(range(rows - kh // 2 + 1, rows))
    for r in pat_rows:
        rs = min(max(r - kh // 2, 0), rows - kh)
        dr0 = rs - r + (NA_KH - 1)
        band = jnp.transpose(by_col[:, :, dr0:dr0 + kh], (0, 1, 3, 2, 4))
        pats.append(band.reshape(depth, 4 * GRID_W, kh * GRID_W))
    return jnp.stack(pats, axis=1)


def _neighborhood(proj, seq, layer, cache_k, cache_v, bias):
    t = proj.shape[0]
    nb = t // seq
    rows = seq // GRID_W
    n_ctx = cache_k.shape[2]
    l = layer
    npat = bias.shape[1]
    return pl.pallas_call(
        functools.partial(_na_kernel, rows),
        grid=(nb,),
        in_specs=[
            pl.BlockSpec((seq, 256), lambda b: (b, P_NAQ // 256)),
            pl.BlockSpec((seq, 256), lambda b: (b, P_NAK // 256)),
            pl.BlockSpec((seq, 256), lambda b: (b, P_NAV // 256)),
            pl.BlockSpec((None, None, n_ctx, 256), lambda b: (b, l, 0, 0)),
            pl.BlockSpec((None, None, n_ctx, 256), lambda b: (b, l, 0, 0)),
            _resident((None, npat, 4 * GRID_W, NA_KH * GRID_W), lambda b: (l, 0, 0, 0)),
        ],
        out_specs=pl.BlockSpec((seq, 256), lambda b: (b, 0)),
        out_shape=jax.ShapeDtypeStruct((t, 256), BF16),
        compiler_params=_params("arbitrary"),
        name="attn_neighborhood_latent",
    )(proj, proj, proj, cache_k, cache_v, bias)


def _mla_cache_kernel(ckv_ref, kr_ref, wkp_ref, wv_ref, gmk_ref, k_ref, v_ref):
    ckv_b = ckv_ref[...].astype(BF16)
    mk_pre = _mm(ckv_b, wkp_ref[...])
    krb = kr_ref[...]
    gmk = gmk_ref[...]
    for hd in range(4):
        lo = hd * MLA_HEAD_PAD
        kb = mk_pre[:, lo:lo + MLA_HEAD_PAD] + krb
        kb = kb * lax.rsqrt(jnp.sum(kb * kb, axis=-1, keepdims=True) * (1.0 / MLA_QK) + EPS)
        k_ref[:, lo:lo + MLA_HEAD_PAD] = (kb * gmk[:, lo:lo + MLA_HEAD_PAD]).astype(BF16)
    v_ref[...] = _mm(ckv_b, wv_ref[...]).T.astype(BF16)


def _mla_cache(cache_ckv, cache_kr_blk, wts):
    nb, depth, n_ctx, _ = cache_ckv.shape
    return pl.pallas_call(
        _mla_cache_kernel,
        grid=(depth, nb),
        in_specs=[
            pl.BlockSpec((None, None, n_ctx, 128), lambda l, b: (b, l, 0, 0)),
            pl.BlockSpec((None, None, n_ctx, 128), lambda l, b: (b, l, 0, 0)),
            pl.BlockSpec((None, 128, 512), lambda l, b: (l, 0, 0)),
            pl.BlockSpec((None, 128, 256), lambda l, b: (l, 0, 0)),
            pl.BlockSpec((None, 1, 512), lambda l, b: (l, 0, 0)),
        ],
        out_specs=[
            pl.BlockSpec((None, None, n_ctx, 512), lambda l, b: (b, l, 0, 0)),
            pl.BlockSpec((None, None, 256, n_ctx), lambda l, b: (b, l, 0, 0)),
        ],
        out_shape=[jax.ShapeDtypeStruct((nb, depth, n_ctx, 512), BF16),
                   jax.ShapeDtypeStruct((nb, depth, 256, n_ctx), BF16)],
        compiler_params=_params("arbitrary", "arbitrary"),
        name="mla_cache_keys",
    )(cache_ckv, cache_kr_blk, wts["w_kp"], wts["w_v"], wts["g_mk"])


def _merge_kernel(x_ref, mod_ref, g1_ref, o0_ref, o1_ref, o2_ref, o3_ref, wg_ref, wb_ref, wo_ref, y_ref):
    x = x_ref[...]
    mod = mod_ref[...]
    sh1, sc1, ga1 = mod[0:1], mod[1:2], mod[2:3]
    hb = (_rms_rows(x) * g1_ref[...] * (1.0 + sc1) + sh1).astype(BF16)
    y = None
    for m, o_ref in enumerate((o0_ref, o1_ref, o2_ref, o3_ref)):
        gate = jax.nn.sigmoid(_mm(hb, wg_ref[:, m * D_MODEL:(m + 1) * D_MODEL]))
        term = gate * _mm(o_ref[...], wb_ref[m])
        y = term if y is None else y + term
    y_ref[...] = x + ga1 * _mm(y.astype(BF16), wo_ref[...])


def _merge(x, mods, outs, wts, layer, *, seq, cond_row0, per_batch, tm):
    t = x.shape[0]
    tiles_per_seq = seq // tm
    l = layer

    def tok(i):
        return (i, 0)

    def modmap(i):
        return (l, (cond_row0 + i // tiles_per_seq) if per_batch else cond_row0, 0, 0)

    return pl.pallas_call(
        _merge_kernel,
        grid=(t // tm,),
        in_specs=[
            pl.BlockSpec((tm, D_MODEL), tok),
            pl.BlockSpec((None, None, N_MOD, D_MODEL), modmap),
            pl.BlockSpec((None, 1, D_MODEL), lambda i: (l, 0, 0)),
        ] + [pl.BlockSpec((tm, 256), tok)] * 4 + [
            _resident((None, D_MODEL, 4 * D_MODEL), lambda i: (l, 0, 0)),
            _resident((None, 4, 256, D_MODEL), lambda i: (l, 0, 0, 0)),
            _resident((None, D_MODEL, D_MODEL), lambda i: (l, 0, 0)),
        ],
        out_specs=pl.BlockSpec((tm, D_MODEL), tok),
        out_shape=jax.ShapeDtypeStruct((t, D_MODEL), F32),
        compiler_params=_params("arbitrary"),
        name="merge",
    )(x, mods, wts["norm1_g"], *outs, wts["w_g"], wts["w_b"], wts["w_o"])


def _split_bf16(v):
    hi = v.astype(BF16)
    lo = (v - hi.astype(F32)).astype(BF16)
    return hi, lo


def _moe_kernel(x_ref, mod_ref, g2_ref, wr_ref, w1_ref, w3_ref, w2_ref, y_ref):
    x = x_ref[...]
    mod = mod_ref[...]
    sh2, sc2, ga2 = mod[3:4], mod[4:5], mod[5:6]
    h = _rms_rows(x) * g2_ref[...] * (1.0 + sc2) + sh2
    hb, hl = _split_bf16(h)
    lg = _mm(hb, wr_ref[...])
    logits = lg[:, :LANE] + (lg[:, LANE:] + _mm(hl, wr_ref[:, :LANE]))
    lane = _lane_iota(logits.shape).astype(F32)
    big = jnp.float32(1e9)

    def first_argmax(v, vmax):
        return jnp.min(jnp.where(v == vmax, lane, big), axis=-1, keepdims=True)

    gl = jnp.where(lane < MOE_GROUPS, logits, NEG)
    gmax = jnp.max(gl, axis=-1, keepdims=True)
    g_top = 1.0 / jnp.sum(jnp.exp(gl - gmax), axis=-1, keepdims=True)
    g_idx = first_argmax(gl, gmax)
    e_lo = EXPERT_LANE0 + MOE_EPG * g_idx
    el = jnp.where((lane >= e_lo) & (lane < e_lo + MOE_EPG), logits, NEG)
    e1 = jnp.max(el, axis=-1, keepdims=True)
    i1 = first_argmax(el, e1)
    el2 = jnp.where(lane == i1, NEG, el)
    e2 = jnp.max(el2, axis=-1, keepdims=True)
    i2 = first_argmax(el2, e2)
    r = jnp.exp(e2 - e1)
    w_1 = 1.0 / (1.0 + r)
    gate = jnp.where(lane == i1, g_top * w_1, jnp.where(lane == i2, g_top * (r * w_1), 0.0))
    experts_per_chunk = MOE_EXPERTS // MOE_CHUNKS
    acc = None
    for c in range(MOE_CHUNKS):
        first = EXPERT_LANE0 + c * experts_per_chunk
        gate_c = jnp.concatenate([jnp.broadcast_to(gate[:, k:k + 1], (gate.shape[0], MOE_FF))
                                  for k in range(first, first + experts_per_chunk)], axis=1)
        cols = slice(c * MOE_CHUNK_W, (c + 1) * MOE_CHUNK_W)
        a = _mm(hb, w1_ref[:, cols])
        hid = (a * jax.nn.sigmoid(a)) * _mm(hb, w3_ref[:, cols]) * gate_c
        d = _mm(hid.astype(BF16), w2_ref[c])
        acc = d if acc is None else acc + d
    y_ref[...] = x + ga2 * acc


def _moe(x, mods, wts, consts, layer, *, seq, cond_row0, per_batch, tm):
    t = x.shape[0]
    tiles_per_seq = seq // tm
    l = layer

    def tok(i):
        return (i, 0)

    def modmap(i):
        return (l, (cond_row0 + i // tiles_per_seq) if per_batch else cond_row0, 0, 0)

    return pl.pallas_call(
        _moe_kernel,
        grid=(t // tm,),
        in_specs=[
            pl.BlockSpec((tm, D_MODEL), tok),
            pl.BlockSpec((None, None, N_MOD, D_MODEL), modmap),
            pl.BlockSpec((None, 1, D_MODEL), lambda i: (l, 0, 0)),
            _resident((None, D_MODEL, 2 * LANE), lambda i: (l, 0, 0)),
            _resident((None, D_MODEL, MOE_EXPERTS * MOE_FF), lambda i: (l, 0, 0)),
            _resident((None, D_MODEL, MOE_EXPERTS * MOE_FF), lambda i: (l, 0, 0)),
            _resident((None, MOE_CHUNKS, MOE_CHUNK_W, D_MODEL), lambda i: (l, 0, 0, 0)),
        ],
        out_specs=pl.BlockSpec((tm, D_MODEL), tok),
        out_shape=jax.ShapeDtypeStruct((t, D_MODEL), F32),
        compiler_params=_params("arbitrary"),
        name="moe",
    )(x, mods, wts["norm2_g"], wts["w_r"], wts["w_1"], wts["w_3"], wts["w_2"])


def _block_ones(n, seg):
    i = np.arange(n)
    return (i[:, None] // seg == i[None, :] // seg).astype(np.float32)


def _rope_perm(n, lo, hi, quarter):
    p = np.zeros((n, n), np.float32)
    for i in range(lo, hi):
        if (i - lo) % (2 * quarter) < quarter:
            p[i + quarter, i] = -1.0
        else:
            p[i - quarter, i] = 1.0
    return p


def _axial_tables(seq, rot_dim):
    tpos = jnp.arange(seq)
    row = (tpos // GRID_W).astype(F32)
    col = (tpos % GRID_W).astype(F32)
    half = rot_dim // 2
    freqs = ROPE_BASE ** (-jnp.arange(0, half, 2, dtype=F32) / half)
    ar, ac = row[:, None] * freqs, col[:, None] * freqs
    ang = jnp.concatenate([ar, ar, ac, ac], axis=-1)
    return jnp.cos(ang), jnp.sin(ang)


def _constants(seq):
    cd, sd = _axial_tables(seq, DIFF_DH)
    cg, sg = _axial_tables(seq, HEAD_DIM)
    cm32, sm32 = _axial_tables(seq, MLA_ROPE)
    ones = jnp.ones((seq, MLA_NOPE), F32)
    zeros = jnp.zeros((seq, MLA_NOPE), F32)
    pad1 = jnp.ones((seq, MLA_HEAD_PAD - MLA_QK), F32)
    pad0 = jnp.zeros((seq, MLA_HEAD_PAD - MLA_QK), F32)
    return dict(
        s64=jnp.asarray(_block_ones(256, 64), BF16),
        s32=jnp.asarray(_block_ones(256, 32), BF16),
        p32=jnp.asarray(_rope_perm(256, 0, 256, DIFF_DH // 4), BF16),
        p64=jnp.asarray(_rope_perm(256, 0, 256, HEAD_DIM // 4), BF16),
        pm=jnp.asarray(_rope_perm(128, MLA_NOPE, MLA_QK, MLA_ROPE // 4), BF16),
        cos_d=jnp.tile(cd, (1, 256 // DIFF_DH)), sin_d=jnp.tile(sd, (1, 256 // DIFF_DH)),
        cos_g=jnp.tile(cg, (1, 256 // HEAD_DIM)), sin_g=jnp.tile(sg, (1, 256 // HEAD_DIM)),
        cos_m=jnp.concatenate([ones, cm32, pad1], axis=-1),
        sin_m=jnp.concatenate([zeros, sm32, pad0], axis=-1),
    )


def _layout_weights(p):
    depth = p["w_in"].shape[0]
    sizes = (256, 256, 256, 256, 256, 256, 256, 128, 32, 256, 128, 128, 4 * D_MODEL)
    cuts = np.concatenate([[0], np.cumsum(sizes)])
    w_in = p["w_in"]

    def seg(i):
        return w_in[:, :, cuts[i]:cuts[i + 1]]

    perm = jnp.array([0, 2, 1, 3])
    gq_w = seg(9).reshape(depth, D_MODEL, 4, HEAD_DIM)[:, :, perm].reshape(depth, D_MODEL, 256)
    zpad = lambda n: jnp.zeros((depth, D_MODEL, n), F32)
    kr_blk = jnp.concatenate([zpad(MLA_NOPE), seg(8), zpad(MLA_HEAD_PAD - MLA_QK)], axis=-1)
    w_a = jnp.concatenate([seg(0), seg(1), seg(2), seg(3), seg(4), seg(5), seg(6), seg(7), kr_blk,
                           gq_w, seg(10), seg(11)], axis=-1).astype(BF16)
    w_g = seg(12).astype(BF16)

    ones = lambda n: jnp.ones((depth, n), F32)
    tile = lambda g, n: jnp.tile(g, (1, n))
    g_a = jnp.concatenate([
        tile(p["na_q_g"], 4) * (HEAD_DIM ** -0.5 * LOG2E), tile(p["na_k_g"], 4), ones(256),
        tile(p["diff_q_g"], 8) * (DIFF_DH ** -0.5 * LOG2E), tile(p["diff_k_g"], 8), ones(256),
        p["mla_q_a_g"], p["mla_kv_a_g"], ones(128),
        tile(p["gqa_q_g"], 4) * (HEAD_DIM ** -0.5 * LOG2E), tile(p["gqa_k_g"], 2), ones(128)], axis=-1)[:, None, :]

    def pad_heads(g):
        return jnp.tile(jnp.pad(g, ((0, 0), (0, MLA_HEAD_PAD - MLA_QK))), (1, 4))[:, None, :]

    w_qb = jnp.pad(p["mla_w_q_b"].reshape(depth, 256, 4, MLA_QK),
                   ((0, 0), (0, 0), (0, 0), (0, MLA_HEAD_PAD - MLA_QK))).reshape(depth, 256, 512).astype(BF16)
    kvb = p["mla_w_kv_b"].reshape(depth, 128, 4, 128)
    w_kp = jnp.pad(kvb[..., :MLA_NOPE], ((0, 0), (0, 0), (0, 0), (0, MLA_HEAD_PAD - MLA_NOPE)))
    w_kp = w_kp.reshape(depth, 128, 512).astype(BF16)
    w_v = kvb[..., MLA_NOPE:].reshape(depth, 128, 256).astype(BF16)

    w_b = p["w_branch"]
    w_b3 = w_b[:, 3].reshape(depth, 4, HEAD_DIM, D_MODEL)[:, perm].reshape(depth, 256, D_MODEL)
    w_b = jnp.concatenate([w_b[:, :3], w_b3[:, None]], axis=1).astype(BF16)

    w_r = jnp.zeros((depth, D_MODEL, LANE), F32)
    w_r = w_r.at[:, :, :MOE_GROUPS].set(p["moe_w_group"])
    w_r = w_r.at[:, :, EXPERT_LANE0:EXPERT_LANE0 + MOE_EXPERTS].set(p["moe_w_expert"])
    w_r_hi = w_r.astype(BF16)
    w_r_lo = (w_r - w_r_hi.astype(F32)).astype(BF16)
    w_r = jnp.concatenate([w_r_hi, w_r_lo], axis=-1)

    return dict(
        norm1_g=p["norm1_g"][:, None, :], norm2_g=p["norm2_g"][:, None, :],
        w_a=w_a, g_a=g_a, w_g=w_g, w_qb=w_qb, w_kp=w_kp, w_v=w_v,
        g_mq=pad_heads(p["mla_q_g"]) * (MLA_QK ** -0.5 * LOG2E), g_mk=pad_heads(p["mla_k_g"]),
        w_b=w_b, w_o=p["w_out"].astype(BF16),
        w_r=w_r,
        w_1=p["moe_w1"].astype(BF16), w_3=p["moe_w3"].astype(BF16),
        w_2=p["moe_w2"].reshape(depth, MOE_CHUNKS, MOE_CHUNK_W, D_MODEL).astype(BF16),
    )


def _pick_tile(n, pref):
    t = min(n, pref)
    while n % t:
        t //= 2
    return t


def kernel(x_prompt, x_sample, cache_na_k, cache_na_v, cache_diff_k, cache_diff_v, cache_mla_ckv, cache_mla_krope, cache_gqa_k, cache_gqa_v, c, c_ctx, w_mod, b_mod, norm1_g, norm2_g, w_in, na_q_g, na_k_g, na_rpb, diff_q_g, diff_k_g, diff_lq1, diff_lk1, diff_lq2, diff_lk2, diff_sub_g, mla_q_a_g, mla_w_q_b, mla_kv_a_g, mla_w_kv_b, mla_q_g, mla_k_g, gqa_q_g, gqa_k_g, w_branch, w_out, moe_w_group, moe_w_expert, moe_w1, moe_w3, moe_w2):
    nbp, seq_p, _ = x_prompt.shape
    nbs, seq_s, _ = x_sample.shape
    depth = w_in.shape[0]
    n_ctx = cache_na_k.shape[2]
    rows = seq_s // GRID_W

    wts = _layout_weights(dict(
        w_in=w_in, na_q_g=na_q_g, na_k_g=na_k_g, diff_q_g=diff_q_g, diff_k_g=diff_k_g,
        mla_q_a_g=mla_q_a_g, mla_kv_a_g=mla_kv_a_g, gqa_q_g=gqa_q_g, gqa_k_g=gqa_k_g,
        mla_w_q_b=mla_w_q_b, mla_w_kv_b=mla_w_kv_b, mla_q_g=mla_q_g, mla_k_g=mla_k_g,
        w_branch=w_branch, w_out=w_out, moe_w_group=moe_w_group, moe_w_expert=moe_w_expert,
        moe_w1=moe_w1, moe_w3=moe_w3, moe_w2=moe_w2, norm1_g=norm1_g, norm2_g=norm2_g))
    consts = _constants(seq_s)

    n_cond = nbs + 1
    cond = jnp.concatenate([c, c_ctx[None, :], jnp.zeros((-n_cond % 8, D_MODEL), F32)], axis=0)
    mods = _modulation(cond, w_mod, b_mod).reshape(depth, cond.shape[0], N_MOD, D_MODEL)

    flat = lambda a: a.reshape(a.shape[0], a.shape[1], a.shape[2], -1).astype(BF16)
    flat_t = lambda a: jnp.swapaxes(flat(a), 2, 3)
    c_na_k, c_na_v = flat(cache_na_k), flat(cache_na_v)
    c_df_k, c_df_vt = flat(cache_diff_k), flat_t(cache_diff_v)
    c_g_k, c_g_vt = flat(cache_gqa_k), flat_t(cache_gqa_v)
    kr_blk = jnp.pad(cache_mla_krope, ((0, 0), (0, 0), (0, 0), (MLA_NOPE, MLA_HEAD_PAD - MLA_QK)))
    c_m_k, c_m_vt = _mla_cache(cache_mla_ckv, kr_blk, wts)

    gsub = jnp.tile(diff_sub_g, (1, 4))[:, None, :]
    lam_args = [a[:, None, :] for a in (diff_lq1, diff_lk1, diff_lq2, diff_lk2)]

    xp = x_prompt.reshape(nbp * seq_p, D_MODEL)
    xs = x_sample.reshape(nbs * seq_s, D_MODEL)
    tm_p = _pick_tile(seq_p, 512)
    tm_s = _pick_tile(seq_s, 512)
    states = []
    bias = _na_bias(na_rpb, rows)
    for l in range(depth):
        lam_init = 0.8 - 0.6 * math.exp(-0.3 * l)
        diff_args = lam_args + [gsub, consts["s64"]]

        proj, vt, state = _project(xp, mods, wts, consts, l, rope=False, seq=seq_p, cond_row0=nbs, tm=tm_p)
        outs = [
            _attend("heads64", proj, vt, seq_p, l, tq=tm_p),
            _attend("diff", proj, vt, seq_p, l, diff_args=diff_args, lam_init=lam_init, tq=tm_p),
            _attend("mla", proj, vt, seq_p, l, tq=tm_p),
            _attend("gqa", proj, vt, seq_p, l, tq=tm_p),
        ]
        xp = _merge(xp, mods, outs, wts, l, seq=seq_p, cond_row0=nbs, per_batch=False, tm=tm_p)
        xp = _moe(xp, mods, wts, consts, l, seq=seq_p, cond_row0=nbs, per_batch=False, tm=tm_p)
        states.append(state.reshape(nbp, seq_p, S_TOTAL))

        proj, vt = _project(xs, mods, wts, consts, l, rope=True, seq=seq_s, cond_row0=0, tm=tm_s)
        outs = [
            _neighborhood(proj, seq_s, l, c_na_k, c_na_v, bias),
            _attend("diff", proj, vt, seq_s, l, cache_k=c_df_k, cache_vt=c_df_vt, diff_args=diff_args,
                    lam_init=lam_init),
            _attend("mla", proj, vt, seq_s, l, cache_k=c_m_k, cache_vt=c_m_vt),
            _attend("gqa", proj, vt, seq_s, l, cache_k=c_g_k, cache_vt=c_g_vt),
        ]
        xs = _merge(xs, mods, outs, wts, l, seq=seq_s, cond_row0=0, per_batch=True, tm=tm_s)
        xs = _moe(xs, mods, wts, consts, l, seq=seq_s, cond_row0=0, per_batch=True, tm=tm_s)

    st = jnp.stack(states, axis=1)
    heads = lambda a, h: a.reshape(nbp, depth, seq_p, h, HEAD_DIM)
    return (
        xp.reshape(nbp, seq_p, D_MODEL),
        xs.reshape(nbs, seq_s, D_MODEL),
        heads(st[..., 0:256], 4), heads(st[..., 256:512], 4),
        heads(st[..., 512:768], 4), heads(st[..., 768:1024], 4),
        st[..., 1024:1152], st[..., 1152 + MLA_NOPE:1152 + MLA_QK],
        heads(st[..., 1280:1408], 2), heads(st[..., 1408:1536], 2),
    )
```

```python
import functools
import math

import numpy as np
import jax
import jax.numpy as jnp
from jax import lax
from jax.experimental import pallas as pl
from jax.experimental.pallas import tpu as pltpu

F32 = jnp.float32
BF16 = jnp.bfloat16

D_MODEL = 1024
GRID_W = 64
HEAD_DIM = 64
NA_KH = 8
NA_KW = 16
DIFF_DH = 32
MLA_HEAD_PAD = 128
MLA_QK = 96
MLA_NOPE = 64
MLA_ROPE = 32
MOE_GROUPS = 4
MOE_EPG = 8
MOE_EXPERTS = 32
MOE_FF = 128
N_MOD = 6
ROPE_BASE = 10000.0
EPS = 1e-6
NEG = -1e30
LANE = 128
EXPERT_LANE0 = 32
MOE_CHUNKS = 8
MOE_CHUNK_W = MOE_EXPERTS * MOE_FF // MOE_CHUNKS

P_NAQ, P_NAK, P_NAV = 0, 256, 512
P_DFQ, P_DFK, P_GQ = 768, 1024, 1280
P_MQ, P_MK, P_GK = 1536, 2048, 2560
P_TOTAL = 2688
VT_ROWS_LATENT = 640
VT_ROWS_CONTEXT = 896
LOG2E = math.log2(math.e)
S_TOTAL = 1536

Z_TOTAL = 2560
VMEM_LIMIT = 56 * 1024 * 1024


def _mm(a, b):
    return jnp.dot(a, b, preferred_element_type=F32)


def _nt(a, b):
    return lax.dot_general(a, b, (((1,), (1,)), ((), ())), preferred_element_type=F32)


def _resident(shape, index_map):
    return pl.BlockSpec(shape, index_map, pipeline_mode=pl.Buffered(1))


def _params(*sem):
    return pltpu.CompilerParams(dimension_semantics=sem, vmem_limit_bytes=VMEM_LIMIT)


def _lane_iota(shape):
    return lax.broadcasted_iota(jnp.int32, shape, len(shape) - 1)


def _rms_rows(x):
    return x * lax.rsqrt(jnp.mean(x * x, axis=-1, keepdims=True) + EPS)


def _seg_norm(v, ones_bd, n):
    ms = _mm((v * v).astype(BF16), ones_bd) * (1.0 / n)
    return v * lax.rsqrt(ms + EPS)


def _mod_kernel(c_ref, w_ref, b_ref, o_ref):
    c = c_ref[...]
    s = c * jax.nn.sigmoid(c)
    o_ref[...] = _mm(s.astype(BF16), w_ref[...].astype(BF16)) + b_ref[...]


def _modulation(cond, w_mod, b_mod):
    depth = w_mod.shape[0]
    rows = cond.shape[0]
    nblk = N_MOD
    return pl.pallas_call(
        _mod_kernel,
        grid=(depth, nblk),
        in_specs=[
            pl.BlockSpec((rows, D_MODEL), lambda l, j: (0, 0)),
            pl.BlockSpec((None, D_MODEL, D_MODEL), lambda l, j: (l, 0, j)),
            pl.BlockSpec((None, 1, D_MODEL), lambda l, j: (l, 0, j)),
        ],
        out_specs=pl.BlockSpec((None, rows, D_MODEL), lambda l, j: (l, 0, j)),
        out_shape=jax.ShapeDtypeStruct((depth, rows, N_MOD * D_MODEL), F32),
        compiler_params=_params("arbitrary", "arbitrary"),
        name="modulation",
    )(cond, w_mod, b_mod.reshape(depth, 1, N_MOD * D_MODEL))


def _proj_kernel(rope, *refs):
    it = iter(refs)
    x_ref, mod_ref, g1_ref, wa_ref, ga_ref = (next(it) for _ in range(5))
    wqb_ref, gmq_ref, wkp_ref, wv_ref, gmk_ref = (next(it) for _ in range(5))
    s64_ref, s32_ref = next(it), next(it)
    if rope:
        p32_ref, p64_ref, pm_ref = next(it), next(it), next(it)
        cd_ref, sd_ref, cg_ref, sg_ref, cm_ref, sm_ref = (next(it) for _ in range(6))
    proj_ref = next(it)
    vt_ref = next(it)
    state_ref = None if rope else next(it)

    x = x_ref[...]
    mod = mod_ref[...]
    sh1, sc1 = mod[0:1], mod[1:2]
    h = _rms_rows(x) * g1_ref[...] * (1.0 + sc1) + sh1
    z = _mm(h.astype(BF16), wa_ref[...])
    ga = ga_ref[...]
    s64 = s64_ref[...]
    s32 = s32_ref[...]

    def seg(lo, w):
        return z[:, lo:lo + w]

    def gain(lo, w):
        return ga[:, lo:lo + w]

    def rot(v, perm, cos, sin):
        return v * cos + _mm(v.astype(BF16), perm) * sin

    naq = _seg_norm(seg(0, 256), s64, 64) * gain(0, 256)
    nak = _seg_norm(seg(256, 256), s64, 64) * gain(256, 256)
    nav = seg(512, 256)
    dfq = _seg_norm(seg(768, 256), s32, 32) * gain(768, 256)
    dfk = _seg_norm(seg(1024, 256), s32, 32) * gain(1024, 256)
    dfv = seg(1280, 256)
    cq = _rms_rows(seg(1536, 256)) * gain(1536, 256)
    ckv = _rms_rows(seg(1792, 128)) * gain(1792, 128)
    krb = seg(1920, 128)
    mq_pre = _mm(cq.astype(BF16), wqb_ref[...])
    ckv_b = ckv.astype(BF16)
    mk_pre = _mm(ckv_b, wkp_ref[...])
    mv = _mm(ckv_b, wv_ref[...])
    gq = _seg_norm(seg(2048, 256), s64, 64) * gain(2048, 256)
    gk = _seg_norm(seg(2304, 128), s64[:128, :128], 64) * gain(2304, 128)
    gv = seg(2432, 128)

    if not rope:
        state_ref[:, 0:256] = nak
        state_ref[:, 256:512] = nav
        state_ref[:, 512:768] = dfk
        state_ref[:, 768:1024] = dfv
        state_ref[:, 1024:1152] = ckv
        state_ref[:, 1152:1280] = krb
        state_ref[:, 1280:1408] = gk
        state_ref[:, 1408:1536] = gv
    else:
        p32, p64 = p32_ref[...], p64_ref[...]
        cd, sd, cg, sg = cd_ref[...], sd_ref[...], cg_ref[...], sg_ref[...]
        dfq = rot(dfq, p32, cd, sd)
        dfk = rot(dfk, p32, cd, sd)
        gq = rot(gq, p64, cg, sg)
        gk = rot(gk, p64[:128, :128], cg[:, :128], sg[:, :128])

    proj_ref[:, P_NAQ:P_NAQ + 256] = naq.astype(BF16)
    proj_ref[:, P_NAK:P_NAK + 256] = nak.astype(BF16)
    proj_ref[:, P_NAV:P_NAV + 256] = nav.astype(BF16)
    proj_ref[:, P_DFQ:P_DFQ + 256] = dfq.astype(BF16)
    proj_ref[:, P_DFK:P_DFK + 256] = dfk.astype(BF16)
    gmq = gmq_ref[...]
    gmk = gmk_ref[...]
    for hd in range(4):
        lo = hd * MLA_HEAD_PAD
        qb = mq_pre[:, lo:lo + MLA_HEAD_PAD]
        qb = qb * lax.rsqrt(jnp.sum(qb * qb, axis=-1, keepdims=True) * (1.0 / MLA_QK) + EPS)
        qb = qb * gmq[:, lo:lo + MLA_HEAD_PAD]
        kb = mk_pre[:, lo:lo + MLA_HEAD_PAD] + krb
        kb = kb * lax.rsqrt(jnp.sum(kb * kb, axis=-1, keepdims=True) * (1.0 / MLA_QK) + EPS)
        kb = kb * gmk[:, lo:lo + MLA_HEAD_PAD]
        if rope:
            pm, cm, sm = pm_ref[...], cm_ref[...], sm_ref[...]
            qb = rot(qb, pm, cm, sm)
            kb = rot(kb, pm, cm, sm)
        proj_ref[:, P_MQ + lo:P_MQ + lo + MLA_HEAD_PAD] = qb.astype(BF16)
        proj_ref[:, P_MK + lo:P_MK + lo + MLA_HEAD_PAD] = kb.astype(BF16)
    proj_ref[:, P_GQ:P_GQ + 256] = gq.astype(BF16)
    proj_ref[:, P_GK:P_GK + 128] = gk.astype(BF16)
    base = 0
    if not rope:
        vt_ref[0:256, :] = nav.T.astype(BF16)
        base = 256
    vt_ref[base:base + 256, :] = dfv.T.astype(BF16)
    vt_ref[base + 256:base + 512, :] = mv.T.astype(BF16)
    vt_ref[base + 512:base + 640, :] = gv.T.astype(BF16)


def _project(x, mods, wts, consts, layer, *, rope, seq, cond_row0, tm):
    t = x.shape[0]
    tiles_per_seq = seq // tm
    nb = t // seq
    l = layer

    def tok(i, b):
        return (b * tiles_per_seq + i, 0)

    def cst(i, b):
        return (0, 0)

    def lay(i, b):
        return (l, 0, 0)

    def modmap(i, b):
        return (l, (cond_row0 + b) if rope else cond_row0, 0, 0)

    in_specs = [
        pl.BlockSpec((tm, D_MODEL), tok),
        pl.BlockSpec((None, None, N_MOD, D_MODEL), modmap),
        pl.BlockSpec((None, 1, D_MODEL), lay),
        _resident((None, D_MODEL, Z_TOTAL), lay),
        pl.BlockSpec((None, 1, Z_TOTAL), lay),
        _resident((None, 256, 512), lay),
        pl.BlockSpec((None, 1, 512), lay),
        _resident((None, 128, 512), lay),
        _resident((None, 128, 256), lay),
        pl.BlockSpec((None, 1, 512), lay),
        _resident((256, 256), cst),
        _resident((256, 256), cst),
    ]
    args = [x, mods, wts["norm1_g"], wts["w_a"], wts["g_a"], wts["w_qb"], wts["g_mq"],
            wts["w_kp"], wts["w_v"], wts["g_mk"], consts["s64"], consts["s32"]]
    if rope:
        def pos(i, b):
            return (i, 0)
        in_specs += [_resident((256, 256), cst), _resident((256, 256), cst), _resident((128, 128), cst)]
        in_specs += [pl.BlockSpec((tm, 256), pos)] * 4 + [pl.BlockSpec((tm, 128), pos)] * 2
        args += [consts["p32"], consts["p64"], consts["pm"], consts["cos_d"], consts["sin_d"],
                 consts["cos_g"], consts["sin_g"], consts["cos_m"], consts["sin_m"]]
    vt_rows = VT_ROWS_LATENT if rope else VT_ROWS_CONTEXT
    out_specs = [pl.BlockSpec((tm, P_TOTAL), tok),
                 pl.BlockSpec((None, None, vt_rows, tm), lambda i, b: (b, i, 0, 0))]
    out_shape = [jax.ShapeDtypeStruct((t, P_TOTAL), BF16),
                 jax.ShapeDtypeStruct((nb, tiles_per_seq, vt_rows, tm), BF16)]
    if not rope:
        out_specs.append(pl.BlockSpec((tm, S_TOTAL), tok))
        out_shape.append(jax.ShapeDtypeStruct((t, S_TOTAL), F32))
    return pl.pallas_call(
        functools.partial(_proj_kernel, rope),
        grid=(tiles_per_seq, nb),
        in_specs=in_specs,
        out_specs=out_specs,
        out_shape=out_shape,
        compiler_params=_params("arbitrary", "arbitrary"),
        name="project_latent" if rope else "project_context",
    )(*args)


_BRANCHES = {
    "heads64": [(0, 256, 64 * h, 64 * h + 64, 0, 256, 64 * h) for h in range(4)],
    "diff": [(0, 256, 32 * (2 * h + i), 32 * (2 * h + i) + 32, 0, 256, 64 * h) for h in range(4) for i in range(2)],
    "mla": [(128 * h, 128, None, None, 128 * h, 128, 64 * h) for h in range(4)],
    "gqa": [(128 * g, 128, 64 * j, 64 * j + 64, 0, 128, 64 * j) for g in range(2) for j in range(2)],
}
ONES_ROWS = 16
ACC_ROWS = HEAD_DIM + ONES_ROWS
SCORE_LOOKAHEAD = {8: 3, 4: 2}
CHUNK_TILES = 8
EXP2_SAFE_RANGE = 64.0


def _flash_kernel(kind, n_chunks, has_ctx, lam_init, *refs):
    maps = _BRANCHES[kind]
    it = iter(refs)
    q_ref, k_ref, vt_ref = next(it), next(it), next(it)
    if has_ctx:
        kctx_ref, vtctx_ref = next(it), next(it)
    if kind == "diff":
        lq1_ref, lk1_ref, lq2_ref, lk2_ref, gsub_ref, s64_ref = (next(it) for _ in range(6))
    o_ref = next(it)
    tq = q_ref.shape[0]
    kc = vt_ref.shape[-1]

    q = q_ref[...]
    qm = []
    for (qlo, qw, mlo, mhi, _, _, _) in maps:
        qj = q[:, qlo:qlo + qw]
        if mlo is not None:
            lane = _lane_iota(qj.shape)
            qj = jnp.where((lane >= mlo) & (lane < mhi), qj, jnp.zeros_like(qj))
        qm.append(qj)

    def chunk(kch, vtch, state, fixed_ref):
        ones = jnp.ones((ONES_ROWS, kch.shape[0]), BF16)

        def scores(j):
            klo, kw = maps[j][4], maps[j][5]
            return _nt(kch[:, klo:klo + kw], qm[j])

        ahead = SCORE_LOOKAHEAD[len(maps)]
        new_maps = []
        pending = [scores(j) for j in range(ahead)]
        for j, ((m_prev, acc), (_, _, _, _, _, _, vrow)) in enumerate(zip(state, maps)):
            st = pending.pop(0)
            if j + ahead < len(maps):
                pending.append(scores(j + ahead))
            vaug = jnp.concatenate([vtch[vrow:vrow + HEAD_DIM, :], ones], axis=0)
            m_new = jnp.maximum(m_prev, jnp.max(st, axis=0, keepdims=True))
            if fixed_ref:
                acc = acc + _mm(vaug, jnp.exp2(st).astype(BF16))
            else:
                pt = jnp.exp2(st - m_new).astype(BF16)
                acc = acc * jnp.exp2(m_prev - m_new) + _mm(vaug, pt)
            new_maps.append((m_new, acc))
        return tuple(new_maps)

    def finish(state):
        heads = [acc[0:HEAD_DIM] * (1.0 / acc[HEAD_DIM:HEAD_DIM + 1]) for _, acc in state]
        if kind == "diff":
            lam = (jnp.exp(jnp.sum(lq1_ref[...] * lk1_ref[...], axis=-1, keepdims=True))
                   - jnp.exp(jnp.sum(lq2_ref[...] * lk2_ref[...], axis=-1, keepdims=True)) + lam_init)
            heads = [heads[2 * h] - lam * heads[2 * h + 1] for h in range(4)]
        o = jnp.concatenate(heads, axis=0).T
        if kind == "diff":
            o = _seg_norm(o, s64_ref[...], 64) * gsub_ref[...] * (1.0 - lam_init)
        o_ref[...] = o.astype(BF16)

    def sweep(state, fixed_ref):
        state = chunk(kctx_ref[...], vtctx_ref[...], state, fixed_ref)
        g = math.gcd(CHUNK_TILES, n_chunks)
        span = g * kc

        def body(c, st):
            vtch = jnp.concatenate([vt_ref[c * g + i] for i in range(g)], axis=1)
            return chunk(k_ref[pl.ds(pl.multiple_of(c * span, span), span), :], vtch, st, fixed_ref)
        return lax.fori_loop(0, n_chunks // g, body, state)

    state = tuple((jnp.full((1, tq), NEG, F32), jnp.zeros((ACC_ROWS, tq), F32)) for _ in maps)
    if not has_ctx:
        assert n_chunks == 1
        finish(chunk(k_ref[...], vt_ref[0], state, False))
    else:
        fast = sweep(state, True)
        row_max = [m for m, _ in fast]
        hi = functools.reduce(jnp.maximum, row_max)
        lo = functools.reduce(jnp.minimum, row_max)
        unsafe = jnp.logical_or(jnp.max(hi) > EXP2_SAFE_RANGE, jnp.min(lo) < -EXP2_SAFE_RANGE)

        @pl.when(jnp.logical_not(unsafe))
        def _():
            finish(fast)

        @pl.when(unsafe)
        def _():
            finish(sweep(state, False))


_BRANCH_COLS = {
    "heads64": (P_NAQ, 256, P_NAK, 256, -256, 256),
    "diff": (P_DFQ, 256, P_DFK, 256, 0, 256),
    "mla": (P_MQ, 512, P_MK, 512, 256, 256),
    "gqa": (P_GQ, 256, P_GK, 128, 512, 128),
}


def _attend(kind, proj, vt, seq, layer, *, cache_k=None, cache_vt=None, diff_args=None, lam_init=0.0, tq=256):
    t = proj.shape[0]
    nb, n_chunks, vt_rows, kc = vt.shape
    nq = seq // tq
    qo, qw, ko, kw, vo, vw = _BRANCH_COLS[kind]
    vo += vt_rows - VT_ROWS_LATENT
    maps = _BRANCHES[kind]
    has_ctx = cache_k is not None
    l = layer

    in_specs = [
        pl.BlockSpec((tq, qw), lambda b, i: (b * nq + i, qo // qw)),
        pl.BlockSpec((seq, kw), lambda b, i: (b, ko // kw)),
        pl.BlockSpec((None, n_chunks, vw, kc), lambda b, i: (b, 0, vo // vw, 0)),
    ]
    args = [proj, proj, vt]
    if has_ctx:
        n_ctx = cache_k.shape[2]
        in_specs += [
            pl.BlockSpec((None, None, n_ctx, kw), lambda b, i: (b, l, 0, 0)),
            pl.BlockSpec((None, None, vw, n_ctx), lambda b, i: (b, l, 0, 0)),
        ]
        args += [cache_k, cache_vt]
    if kind == "diff":
        in_specs += [pl.BlockSpec((None, 1, DIFF_DH), lambda b, i: (l, 0, 0))] * 4
        in_specs += [pl.BlockSpec((None, 1, 256), lambda b, i: (l, 0, 0)),
                     pl.BlockSpec((256, 256), lambda b, i: (0, 0))]
        args += list(diff_args)
    return pl.pallas_call(
        functools.partial(_flash_kernel, kind, n_chunks, has_ctx, lam_init),
        grid=(nb, nq),
        in_specs=in_specs,
        out_specs=pl.BlockSpec((tq, 256), lambda b, i: (b * nq + i, 0)),
        out_shape=jax.ShapeDtypeStruct((t, 256), BF16),
        compiler_params=_params("arbitrary", "arbitrary"),
        name="attn_" + kind + ("_latent" if has_ctx else "_context"),
    )(*args)


NA_ROWS_PER_STEP = 4
NA_LOOKAHEAD = 2


def _na_kernel(rows, q_ref, k_ref, v_ref, kc_ref, vc_ref, bias_ref, o_ref):
    kh = NA_KH
    band = kh * GRID_W
    lane = _lane_iota((GRID_W, 256))
    head_masks = [(lane >= 64 * h) & (lane < 64 * h + 64) for h in range(4)]
    kctx = kc_ref[...]
    vctx = vc_ref[...]

    def scores(r):
        rs = jnp.clip(r - kh // 2, 0, rows - kh)
        pat = jnp.where(r < kh // 2, r, jnp.where(r > rows - kh // 2, r - (rows - kh), kh // 2))
        q = q_ref[pl.ds(pl.multiple_of(r * GRID_W, GRID_W), GRID_W), :]
        qs = jnp.concatenate([jnp.where(mk, q, jnp.zeros_like(q)) for mk in head_masks], axis=0)
        start = pl.multiple_of(rs * GRID_W, GRID_W)
        sb = _nt(qs, k_ref[pl.ds(start, band), :]) + bias_ref[pat]
        sc = _nt(qs, kctx)
        return r, start, sb, sc

    def finish(r, start, sb, sc):
        m = jnp.maximum(jnp.max(sb, axis=-1, keepdims=True), jnp.max(sc, axis=-1, keepdims=True))
        pb = jnp.exp2(sb - m)
        pc = jnp.exp2(sc - m)
        den = jnp.sum(pb, axis=-1, keepdims=True) + jnp.sum(pc, axis=-1, keepdims=True)
        vb = v_ref[pl.ds(start, band), :]
        of = (_mm(pb.astype(BF16), vb) + _mm(pc.astype(BF16), vctx)) * (1.0 / den)
        o = jnp.zeros((GRID_W, 256), F32)
        for h in range(4):
            o = jnp.where(head_masks[h], of[GRID_W * h:GRID_W * (h + 1), :], o)
        o_ref[pl.ds(pl.multiple_of(r * GRID_W, GRID_W), GRID_W), :] = o.astype(BF16)

    def body(i, carry):
        todo = [i * NA_ROWS_PER_STEP + t for t in range(NA_ROWS_PER_STEP)]
        pending = [scores(r) for r in todo[:NA_LOOKAHEAD]]
        for t in range(NA_ROWS_PER_STEP):
            cur = pending.pop(0)
            if t + NA_LOOKAHEAD < NA_ROWS_PER_STEP:
                pending.append(scores(todo[t + NA_LOOKAHEAD]))
            finish(*cur)
        return carry

    lax.fori_loop(0, rows // NA_ROWS_PER_STEP, body, 0)


def _na_bias(rpb, rows):
    kh, kw = NA_KH, NA_KW
    depth = rpb.shape[0]
    col = np.arange(GRID_W)
    cs = np.clip(col - kw // 2, 0, GRID_W - kw)
    col_ok = (col[None, :] >= cs[:, None]) & (col[None, :] < cs[:, None] + kw)
    dc_idx = np.clip(col[None, :] - col[:, None], -(kw - 1), kw - 1) + (NA_KW - 1)
    by_col = jnp.take(rpb * LOG2E, jnp.asarray(dc_idx.reshape(-1)), axis=-1)
    by_col = by_col.reshape(depth, 4, 2 * NA_KH - 1, GRID_W, GRID_W)
    by_col = jnp.where(col_ok[None, None, None], by_col, NEG)
    pats = []
    pat_rows = list(range(kh // 2)) + [kh // 2] + list(range(rows - kh // 2 + 1, rows))
    for r in pat_rows:
        rs = min(max(r - kh // 2, 0), rows - kh)
        dr0 = rs - r + (NA_KH - 1)
        band = jnp.transpose(by_col[:, :, dr0:dr0 + kh], (0, 1, 3, 2, 4))
        pats.append(band.reshape(depth, 4 * GRID_W, kh * GRID_W))
    return jnp.stack(pats, axis=1)


def _neighborhood(proj, seq, layer, cache_k, cache_v, bias):
    t = proj.shape[0]
    nb = t // seq
    rows = seq // GRID_W
    n_ctx = cache_k.shape[2]
    l = layer
    npat = bias.shape[1]
    return pl.pallas_call(
        functools.partial(_na_kernel, rows),
        grid=(nb,),
        in_specs=[
            pl.BlockSpec((seq, 256), lambda b: (b, P_NAQ // 256)),
            pl.BlockSpec((seq, 256), lambda b: (b, P_NAK // 256)),
            pl.BlockSpec((seq, 256), lambda b: (b, P_NAV // 256)),
            pl.BlockSpec((None, None, n_ctx, 256), lambda b: (b, l, 0, 0)),
            pl.BlockSpec((None, None, n_ctx, 256), lambda b: (b, l, 0, 0)),
            _resident((None, npat, 4 * GRID_W, NA_KH * GRID_W), lambda b: (l, 0, 0, 0)),
        ],
        out_specs=pl.BlockSpec((seq, 256), lambda b: (b, 0)),
        out_shape=jax.ShapeDtypeStruct((t, 256), BF16),
        compiler_params=_params("arbitrary"),
        name="attn_neighborhood_latent",
    )(proj, proj, proj, cache_k, cache_v, bias)


def _mla_cache_kernel(ckv_ref, kr_ref, wkp_ref, wv_ref, gmk_ref, k_ref, v_ref):
    ckv_b = ckv_ref[...].astype(BF16)
    mk_pre = _mm(ckv_b, wkp_ref[...])
    krb = kr_ref[...]
    gmk = gmk_ref[...]
    for hd in range(4):
        lo = hd * MLA_HEAD_PAD
        kb = mk_pre[:, lo:lo + MLA_HEAD_PAD] + krb
        kb = kb * lax.rsqrt(jnp.sum(kb * kb, axis=-1, keepdims=True) * (1.0 / MLA_QK) + EPS)
        k_ref[:, lo:lo + MLA_HEAD_PAD] = (kb * gmk[:, lo:lo + MLA_HEAD_PAD]).astype(BF16)
    v_ref[...] = _mm(ckv_b, wv_ref[...]).T.astype(BF16)


def _mla_cache(cache_ckv, cache_kr_blk, wts):
    nb, depth, n_ctx, _ = cache_ckv.shape
    return pl.pallas_call(
        _mla_cache_kernel,
        grid=(depth, nb),
        in_specs=[
            pl.BlockSpec((None, None, n_ctx, 128), lambda l, b: (b, l, 0, 0)),
            pl.BlockSpec((None, None, n_ctx, 128), lambda l, b: (b, l, 0, 0)),
            pl.BlockSpec((None, 128, 512), lambda l, b: (l, 0, 0)),
            pl.BlockSpec((None, 128, 256), lambda l, b: (l, 0, 0)),
            pl.BlockSpec((None, 1, 512), lambda l, b: (l, 0, 0)),
        ],
        out_specs=[
            pl.BlockSpec((None, None, n_ctx, 512), lambda l, b: (b, l, 0, 0)),
            pl.BlockSpec((None, None, 256, n_ctx), lambda l, b: (b, l, 0, 0)),
        ],
        out_shape=[jax.ShapeDtypeStruct((nb, depth, n_ctx, 512), BF16),
                   jax.ShapeDtypeStruct((nb, depth, 256, n_ctx), BF16)],
        compiler_params=_params("arbitrary", "arbitrary"),
        name="mla_cache_keys",
    )(cache_ckv, cache_kr_blk, wts["w_kp"], wts["w_v"], wts["g_mk"])


def _merge_kernel(x_ref, mod_ref, g1_ref, o0_ref, o1_ref, o2_ref, o3_ref, wg_ref, wb_ref, wo_ref, y_ref):
    x = x_ref[...]
    mod = mod_ref[...]
    sh1, sc1, ga1 = mod[0:1], mod[1:2], mod[2:3]
    hb = (_rms_rows(x) * g1_ref[...] * (1.0 + sc1) + sh1).astype(BF16)
    y = None
    for m, o_ref in enumerate((o0_ref, o1_ref, o2_ref, o3_ref)):
        gate = jax.nn.sigmoid(_mm(hb, wg_ref[:, m * D_MODEL:(m + 1) * D_MODEL]))
        term = gate * _mm(o_ref[...], wb_ref[m])
        y = term if y is None else y + term
    y_ref[...] = x + ga1 * _mm(y.astype(BF16), wo_ref[...])


def _merge(x, mods, outs, wts, layer, *, seq, cond_row0, per_batch, tm):
    t = x.shape[0]
    tiles_per_seq = seq // tm
    l = layer

    def tok(i):
        return (i, 0)

    def modmap(i):
        return (l, (cond_row0 + i // tiles_per_seq) if per_batch else cond_row0, 0, 0)

    return pl.pallas_call(
        _merge_kernel,
        grid=(t // tm,),
        in_specs=[
            pl.BlockSpec((tm, D_MODEL), tok),
            pl.BlockSpec((None, None, N_MOD, D_MODEL), modmap),
            pl.BlockSpec((None, 1, D_MODEL), lambda i: (l, 0, 0)),
        ] + [pl.BlockSpec((tm, 256), tok)] * 4 + [
            _resident((None, D_MODEL, 4 * D_MODEL), lambda i: (l, 0, 0)),
            _resident((None, 4, 256, D_MODEL), lambda i: (l, 0, 0, 0)),
            _resident((None, D_MODEL, D_MODEL), lambda i: (l, 0, 0)),
        ],
        out_specs=pl.BlockSpec((tm, D_MODEL), tok),
        out_shape=jax.ShapeDtypeStruct((t, D_MODEL), F32),
        compiler_params=_params("arbitrary"),
        name="merge",
    )(x, mods, wts["norm1_g"], *outs, wts["w_g"], wts["w_b"], wts["w_o"])


def _split_bf16(v):
    hi = v.astype(BF16)
    lo = (v - hi.astype(F32)).astype(BF16)
    return hi, lo


def _moe_kernel(x_ref, mod_ref, g2_ref, wr_ref, w1_ref, w3_ref, w2_ref, y_ref):
    x = x_ref[...]
    mod = mod_ref[...]
    sh2, sc2, ga2 = mod[3:4], mod[4:5], mod[5:6]
    h = _rms_rows(x) * g2_ref[...] * (1.0 + sc2) + sh2
    hb, hl = _split_bf16(h)
    lg = _mm(hb, wr_ref[...])
    logits = lg[:, :LANE] + (lg[:, LANE:] + _mm(hl, wr_ref[:, :LANE]))
    lane = _lane_iota(logits.shape).astype(F32)
    big = jnp.float32(1e9)

    def first_argmax(v, vmax):
        return jnp.min(jnp.where(v == vmax, lane, big), axis=-1, keepdims=True)

    gl = jnp.where(lane < MOE_GROUPS, logits, NEG)
    gmax = jnp.max(gl, axis=-1, keepdims=True)
    g_top = 1.0 / jnp.sum(jnp.exp(gl - gmax), axis=-1, keepdims=True)
    g_idx = first_argmax(gl, gmax)
    e_lo = EXPERT_LANE0 + MOE_EPG * g_idx
    el = jnp.where((lane >= e_lo) & (lane < e_lo + MOE_EPG), logits, NEG)
    e1 = jnp.max(el, axis=-1, keepdims=True)
    i1 = first_argmax(el, e1)
    el2 = jnp.where(lane == i1, NEG, el)
    e2 = jnp.max(el2, axis=-1, keepdims=True)
    i2 = first_argmax(el2, e2)
    r = jnp.exp(e2 - e1)
    w_1 = 1.0 / (1.0 + r)
    gate = jnp.where(lane == i1, g_top * w_1, jnp.where(lane == i2, g_top * (r * w_1), 0.0))
    experts_per_chunk = MOE_EXPERTS // MOE_CHUNKS
    acc = None
    for c in range(MOE_CHUNKS):
        first = EXPERT_LANE0 + c * experts_per_chunk
        gate_c = jnp.concatenate([jnp.broadcast_to(gate[:, k:k + 1], (gate.shape[0], MOE_FF))
                                  for k in range(first, first + experts_per_chunk)], axis=1)
        cols = slice(c * MOE_CHUNK_W, (c + 1) * MOE_CHUNK_W)
        a = _mm(hb, w1_ref[:, cols])
        hid = (a * jax.nn.sigmoid(a)) * _mm(hb, w3_ref[:, cols]) * gate_c
        d = _mm(hid.astype(BF16), w2_ref[c])
        acc = d if acc is None else acc + d
    y_ref[...] = x + ga2 * acc


def _moe(x, mods, wts, consts, layer, *, seq, cond_row0, per_batch, tm):
    t = x.shape[0]
    tiles_per_seq = seq // tm
    l = layer

    def tok(i):
        return (i, 0)

    def modmap(i):
        return (l, (cond_row0 + i // tiles_per_seq) if per_batch else cond_row0, 0, 0)

    return pl.pallas_call(
        _moe_kernel,
        grid=(t // tm,),
        in_specs=[
            pl.BlockSpec((tm, D_MODEL), tok),
            pl.BlockSpec((None, None, N_MOD, D_MODEL), modmap),
            pl.BlockSpec((None, 1, D_MODEL), lambda i: (l, 0, 0)),
            _resident((None, D_MODEL, 2 * LANE), lambda i: (l, 0, 0)),
            _resident((None, D_MODEL, MOE_EXPERTS * MOE_FF), lambda i: (l, 0, 0)),
            _resident((None, D_MODEL, MOE_EXPERTS * MOE_FF), lambda i: (l, 0, 0)),
            _resident((None, MOE_CHUNKS, MOE_CHUNK_W, D_MODEL), lambda i: (l, 0, 0, 0)),
        ],
        out_specs=pl.BlockSpec((tm, D_MODEL), tok),
        out_shape=jax.ShapeDtypeStruct((t, D_MODEL), F32),
        compiler_params=_params("arbitrary"),
        name="moe",
    )(x, mods, wts["norm2_g"], wts["w_r"], wts["w_1"], wts["w_3"], wts["w_2"])


def _block_ones(n, seg):
    i = np.arange(n)
    return (i[:, None] // seg == i[None, :] // seg).astype(np.float32)


def _rope_perm(n, lo, hi, quarter):
    p = np.zeros((n, n), np.float32)
    for i in range(lo, hi):
        if (i - lo) % (2 * quarter) < quarter:
            p[i + quarter, i] = -1.0
        else:
            p[i - quarter, i] = 1.0
    return p


def _axial_tables(seq, rot_dim):
    tpos = jnp.arange(seq)
    row = (tpos // GRID_W).astype(F32)
    col = (tpos % GRID_W).astype(F32)
    half = rot_dim // 2
    freqs = ROPE_BASE ** (-jnp.arange(0, half, 2, dtype=F32) / half)
    ar, ac = row[:, None] * freqs, col[:, None] * freqs
    ang = jnp.concatenate([ar, ar, ac, ac], axis=-1)
    return jnp.cos(ang), jnp.sin(ang)


def _constants(seq):
    cd, sd = _axial_tables(seq, DIFF_DH)
    cg, sg = _axial_tables(seq, HEAD_DIM)
    cm32, sm32 = _axial_tables(seq, MLA_ROPE)
    ones = jnp.ones((seq, MLA_NOPE), F32)
    zeros = jnp.zeros((seq, MLA_NOPE), F32)
    pad1 = jnp.ones((seq, MLA_HEAD_PAD - MLA_QK), F32)
    pad0 = jnp.zeros((seq, MLA_HEAD_PAD - MLA_QK), F32)
    return dict(
        s64=jnp.asarray(_block_ones(256, 64), BF16),
        s32=jnp.asarray(_block_ones(256, 32), BF16),
        p32=jnp.asarray(_rope_perm(256, 0, 256, DIFF_DH // 4), BF16),
        p64=jnp.asarray(_rope_perm(256, 0, 256, HEAD_DIM // 4), BF16),
        pm=jnp.asarray(_rope_perm(128, MLA_NOPE, MLA_QK, MLA_ROPE // 4), BF16),
        cos_d=jnp.tile(cd, (1, 256 // DIFF_DH)), sin_d=jnp.tile(sd, (1, 256 // DIFF_DH)),
        cos_g=jnp.tile(cg, (1, 256 // HEAD_DIM)), sin_g=jnp.tile(sg, (1, 256 // HEAD_DIM)),
        cos_m=jnp.concatenate([ones, cm32, pad1], axis=-1),
        sin_m=jnp.concatenate([zeros, sm32, pad0], axis=-1),
    )


def _layout_weights(p):
    depth = p["w_in"].shape[0]
    sizes = (256, 256, 256, 256, 256, 256, 256, 128, 32, 256, 128, 128, 4 * D_MODEL)
    cuts = np.concatenate([[0], np.cumsum(sizes)])
    w_in = p["w_in"]

    def seg(i):
        return w_in[:, :, cuts[i]:cuts[i + 1]]

    perm = jnp.array([0, 2, 1, 3])
    gq_w = seg(9).reshape(depth, D_MODEL, 4, HEAD_DIM)[:, :, perm].reshape(depth, D_MODEL, 256)
    zpad = lambda n: jnp.zeros((depth, D_MODEL, n), F32)
    kr_blk = jnp.concatenate([zpad(MLA_NOPE), seg(8), zpad(MLA_HEAD_PAD - MLA_QK)], axis=-1)
    w_a = jnp.concatenate([seg(0), seg(1), seg(2), seg(3), seg(4), seg(5), seg(6), seg(7), kr_blk,
                           gq_w, seg(10), seg(11)], axis=-1).astype(BF16)
    w_g = seg(12).astype(BF16)

    ones = lambda n: jnp.ones((depth, n), F32)
    tile = lambda g, n: jnp.tile(g, (1, n))
    g_a = jnp.concatenate([
        tile(p["na_q_g"], 4) * (HEAD_DIM ** -0.5 * LOG2E), tile(p["na_k_g"], 4), ones(256),
        tile(p["diff_q_g"], 8) * (DIFF_DH ** -0.5 * LOG2E), tile(p["diff_k_g"], 8), ones(256),
        p["mla_q_a_g"], p["mla_kv_a_g"], ones(128),
        tile(p["gqa_q_g"], 4) * (HEAD_DIM ** -0.5 * LOG2E), tile(p["gqa_k_g"], 2), ones(128)], axis=-1)[:, None, :]

    def pad_heads(g):
        return jnp.tile(jnp.pad(g, ((0, 0), (0, MLA_HEAD_PAD - MLA_QK))), (1, 4))[:, None, :]

    w_qb = jnp.pad(p["mla_w_q_b"].reshape(depth, 256, 4, MLA_QK),
                   ((0, 0), (0, 0), (0, 0), (0, MLA_HEAD_PAD - MLA_QK))).reshape(depth, 256, 512).astype(BF16)
    kvb = p["mla_w_kv_b"].reshape(depth, 128, 4, 128)
    w_kp = jnp.pad(kvb[..., :MLA_NOPE], ((0, 0), (0, 0), (0, 0), (0, MLA_HEAD_PAD - MLA_NOPE)))
    w_kp = w_kp.reshape(depth, 128, 512).astype(BF16)
    w_v = kvb[..., MLA_NOPE:].reshape(depth, 128, 256).astype(BF16)

    w_b = p["w_branch"]
    w_b3 = w_b[:, 3].reshape(depth, 4, HEAD_DIM, D_MODEL)[:, perm].reshape(depth, 256, D_MODEL)
    w_b = jnp.concatenate([w_b[:, :3], w_b3[:, None]], axis=1).astype(BF16)

    w_r = jnp.zeros((depth, D_MODEL, LANE), F32)
    w_r = w_r.at[:, :, :MOE_GROUPS].set(p["moe_w_group"])
    w_r = w_r.at[:, :, EXPERT_LANE0:EXPERT_LANE0 + MOE_EXPERTS].set(p["moe_w_expert"])
    w_r_hi = w_r.astype(BF16)
    w_r_lo = (w_r - w_r_hi.astype(F32)).astype(BF16)
    w_r = jnp.concatenate([w_r_hi, w_r_lo], axis=-1)

    return dict(
        norm1_g=p["norm1_g"][:, None, :], norm2_g=p["norm2_g"][:, None, :],
        w_a=w_a, g_a=g_a, w_g=w_g, w_qb=w_qb, w_kp=w_kp, w_v=w_v,
        g_mq=pad_heads(p["mla_q_g"]) * (MLA_QK ** -0.5 * LOG2E), g_mk=pad_heads(p["mla_k_g"]),
        w_b=w_b, w_o=p["w_out"].astype(BF16),
        w_r=w_r,
        w_1=p["moe_w1"].astype(BF16), w_3=p["moe_w3"].astype(BF16),
        w_2=p["moe_w2"].reshape(depth, MOE_CHUNKS, MOE_CHUNK_W, D_MODEL).astype(BF16),
    )


def _pick_tile(n, pref):
    t = min(n, pref)
    while n % t:
        t //= 2
    return t


def kernel(x_prompt, x_sample, cache_na_k, cache_na_v, cache_diff_k, cache_diff_v, cache_mla_ckv, cache_mla_krope, cache_gqa_k, cache_gqa_v, c, c_ctx, w_mod, b_mod, norm1_g, norm2_g, w_in, na_q_g, na_k_g, na_rpb, diff_q_g, diff_k_g, diff_lq1, diff_lk1, diff_lq2, diff_lk2, diff_sub_g, mla_q_a_g, mla_w_q_b, mla_kv_a_g, mla_w_kv_b, mla_q_g, mla_k_g, gqa_q_g, gqa_k_g, w_branch, w_out, moe_w_group, moe_w_expert, moe_w1, moe_w3, moe_w2):
    nbp, seq_p, _ = x_prompt.shape
    nbs, seq_s, _ = x_sample.shape
    depth = w_in.shape[0]
    n_ctx = cache_na_k.shape[2]
    rows = seq_s // GRID_W

    wts = _layout_weights(dict(
        w_in=w_in, na_q_g=na_q_g, na_k_g=na_k_g, diff_q_g=diff_q_g, diff_k_g=diff_k_g,
        mla_q_a_g=mla_q_a_g, mla_kv_a_g=mla_kv_a_g, gqa_q_g=gqa_q_g, gqa_k_g=gqa_k_g,
        mla_w_q_b=mla_w_q_b, mla_w_kv_b=mla_w_kv_b, mla_q_g=mla_q_g, mla_k_g=mla_k_g,
        w_branch=w_branch, w_out=w_out, moe_w_group=moe_w_group, moe_w_expert=moe_w_expert,
        moe_w1=moe_w1, moe_w3=moe_w3, moe_w2=moe_w2, norm1_g=norm1_g, norm2_g=norm2_g))
    consts = _constants(seq_s)

    n_cond = nbs + 1
    cond = jnp.concatenate([c, c_ctx[None, :], jnp.zeros((-n_cond % 8, D_MODEL), F32)], axis=0)
    mods = _modulation(cond, w_mod, b_mod).reshape(depth, cond.shape[0], N_MOD, D_MODEL)

    flat = lambda a: a.reshape(a.shape[0], a.shape[1], a.shape[2], -1).astype(BF16)
    flat_t = lambda a: jnp.swapaxes(flat(a), 2, 3)
    c_na_k, c_na_v = flat(cache_na_k), flat(cache_na_v)
    c_df_k, c_df_vt = flat(cache_diff_k), flat_t(cache_diff_v)
    c_g_k, c_g_vt = flat(cache_gqa_k), flat_t(cache_gqa_v)
    kr_blk = jnp.pad(cache_mla_krope, ((0, 0), (0, 0), (0, 0), (MLA_NOPE, MLA_HEAD_PAD - MLA_QK)))
    c_m_k, c_m_vt = _mla_cache(cache_mla_ckv, kr_blk, wts)

    gsub = jnp.tile(diff_sub_g, (1, 4))[:, None, :]
    lam_args = [a[:, None, :] for a in (diff_lq1, diff_lk1, diff_lq2, diff_lk2)]

    xp = x_prompt.reshape(nbp * seq_p, D_MODEL)
    xs = x_sample.reshape(nbs * seq_s, D_MODEL)
    tm_p = _pick_tile(seq_p, 512)
    tm_s = _pick_tile(seq_s, 512)
    states = []
    bias = _na_bias(na_rpb, rows)
    for l in range(depth):
        lam_init = 0.8 - 0.6 * math.exp(-0.3 * l)
        diff_args = lam_args + [gsub, consts["s64"]]

        proj, vt, state = _project(xp, mods, wts, consts, l, rope=False, seq=seq_p, cond_row0=nbs, tm=tm_p)
        outs = [
            _attend("heads64", proj, vt, seq_p, l, tq=tm_p),
            _attend("diff", proj, vt, seq_p, l, diff_args=diff_args, lam_init=lam_init, tq=tm_p),
            _attend("mla", proj, vt, seq_p, l, tq=tm_p),
            _attend("gqa", proj, vt, seq_p, l, tq=tm_p),
        ]
        xp = _merge(xp, mods, outs, wts, l, seq=seq_p, cond_row0=nbs, per_batch=False, tm=tm_p)
        xp = _moe(xp, mods, wts, consts, l, seq=seq_p, cond_row0=nbs, per_batch=False, tm=tm_p)
        states.append(state.reshape(nbp, seq_p, S_TOTAL))

        proj, vt = _project(xs, mods, wts, consts, l, rope=True, seq=seq_s, cond_row0=0, tm=tm_s)
        outs = [
            _neighborhood(proj, seq_s, l, c_na_k, c_na_v, bias),
            _attend("diff", proj, vt, seq_s, l, cache_k=c_df_k, cache_vt=c_df_vt, diff_args=diff_args,
                    lam_init=lam_init),
            _attend("mla", proj, vt, seq_s, l, cache_k=c_m_k, cache_vt=c_m_vt),
            _attend("gqa", proj, vt, seq_s, l, cache_k=c_g_k, cache_vt=c_g_vt),
        ]
        xs = _merge(xs, mods, outs, wts, l, seq=seq_s, cond_row0=0, per_batch=True, tm=tm_s)
        xs = _moe(xs, mods, wts, consts, l, seq=seq_s, cond_row0=0, per_batch=True, tm=tm_s)

    st = jnp.stack(states, axis=1)
    heads = lambda a, h: a.reshape(nbp, depth, seq_p, h, HEAD_DIM)
    return (
        xp.reshape(nbp, seq_p, D_MODEL),
        xs.reshape(nbs, seq_s, D_MODEL),
        heads(st[..., 0:256], 4), heads(st[..., 256:512], 4),
        heads(st[..., 512:768], 4), heads(st[..., 768:1024], 4),
        st[..., 1024:1152], st[..., 1152 + MLA_NOPE:1152 + MLA_QK],
        heads(st[..., 1280:1408], 2), heads(st[..., 1408:1536], 2),
    )
```

```python
import functools
import math

import numpy as np
import jax
import jax.numpy as jnp
from jax import lax
from jax.experimental import pallas as pl
from jax.experimental.pallas import tpu as pltpu

F32 = jnp.float32
BF16 = jnp.bfloat16

D_MODEL = 1024
GRID_W = 64
HEAD_DIM = 64
NA_KH = 8
NA_KW = 16
DIFF_DH = 32
MLA_HEAD_PAD = 128
MLA_QK = 96
MLA_NOPE = 64
MLA_ROPE = 32
MOE_GROUPS = 4
MOE_EPG = 8
MOE_EXPERTS = 32
MOE_FF = 128
N_MOD = 6
ROPE_BASE = 10000.0
EPS = 1e-6
NEG = -1e30
LANE = 128
EXPERT_LANE0 = 32
MOE_CHUNKS = 8
MOE_CHUNK_W = MOE_EXPERTS * MOE_FF // MOE_CHUNKS

P_NAQ, P_NAK, P_NAV = 0, 256, 512
P_DFQ, P_DFK, P_GQ = 768, 1024, 1280
P_MQ, P_MK, P_GK = 1536, 2048, 2560
P_TOTAL = 2688
VT_ROWS_LATENT = 640
VT_ROWS_CONTEXT = 896
LOG2E = math.log2(math.e)
S_TOTAL = 1536

Z_TOTAL = 2560
VMEM_LIMIT = 56 * 1024 * 1024
FUSED_VMEM_LIMIT = 60 * 1024 * 1024


def _mm(a, b):
    return jnp.dot(a, b, preferred_element_type=F32)


def _nt(a, b):
    return lax.dot_general(a, b, (((1,), (1,)), ((), ())), preferred_element_type=F32)


def _resident(shape, index_map):
    return pl.BlockSpec(shape, index_map, pipeline_mode=pl.Buffered(1))


def _params(*sem):
    return pltpu.CompilerParams(dimension_semantics=sem, vmem_limit_bytes=VMEM_LIMIT)


def _lane_iota(shape):
    return lax.broadcasted_iota(jnp.int32, shape, len(shape) - 1)


def _rms_rows(x):
    return x * lax.rsqrt(jnp.mean(x * x, axis=-1, keepdims=True) + EPS)


def _seg_norm(v, ones_bd, n):
    ms = _mm((v * v).astype(BF16), ones_bd) * (1.0 / n)
    return v * lax.rsqrt(ms + EPS)


def _mod_kernel(c_ref, w_ref, b_ref, o_ref):
    c = c_ref[...]
    s = c * jax.nn.sigmoid(c)
    o_ref[...] = _mm(s.astype(BF16), w_ref[...].astype(BF16)) + b_ref[...]


def _modulation(cond, w_mod, b_mod):
    depth = w_mod.shape[0]
    rows = cond.shape[0]
    nblk = N_MOD
    return pl.pallas_call(
        _mod_kernel,
        grid=(depth, nblk),
        in_specs=[
            pl.BlockSpec((rows, D_MODEL), lambda l, j: (0, 0)),
            pl.BlockSpec((None, D_MODEL, D_MODEL), lambda l, j: (l, 0, j)),
            pl.BlockSpec((None, 1, D_MODEL), lambda l, j: (l, 0, j)),
        ],
        out_specs=pl.BlockSpec((None, rows, D_MODEL), lambda l, j: (l, 0, j)),
        out_shape=jax.ShapeDtypeStruct((depth, rows, N_MOD * D_MODEL), F32),
        compiler_params=_params("arbitrary", "arbitrary"),
        name="modulation",
    )(cond, w_mod, b_mod.reshape(depth, 1, N_MOD * D_MODEL))


def _proj_kernel(rope, *refs):
    it = iter(refs)
    x_ref, mod_ref, g1_ref, wa_ref, ga_ref = (next(it) for _ in range(5))
    wqb_ref, gmq_ref, wkp_ref, wv_ref, gmk_ref = (next(it) for _ in range(5))
    s64_ref, s32_ref = next(it), next(it)
    if rope:
        p32_ref, p64_ref, pm_ref = next(it), next(it), next(it)
        cd_ref, sd_ref, cg_ref, sg_ref, cm_ref, sm_ref = (next(it) for _ in range(6))
    proj_ref = next(it)
    vt_ref = next(it)
    state_ref = None if rope else next(it)

    x = x_ref[...]
    mod = mod_ref[...]
    sh1, sc1 = mod[0:1], mod[1:2]
    h = _rms_rows(x) * g1_ref[...] * (1.0 + sc1) + sh1
    z = _mm(h.astype(BF16), wa_ref[...])
    ga = ga_ref[...]
    s64 = s64_ref[...]
    s32 = s32_ref[...]

    def seg(lo, w):
        return z[:, lo:lo + w]

    def gain(lo, w):
        return ga[:, lo:lo + w]

    def rot(v, perm, cos, sin):
        return v * cos + _mm(v.astype(BF16), perm) * sin

    naq = _seg_norm(seg(0, 256), s64, 64) * gain(0, 256)
    nak = _seg_norm(seg(256, 256), s64, 64) * gain(256, 256)
    nav = seg(512, 256)
    dfq = _seg_norm(seg(768, 256), s32, 32) * gain(768, 256)
    dfk = _seg_norm(seg(1024, 256), s32, 32) * gain(1024, 256)
    dfv = seg(1280, 256)
    cq = _rms_rows(seg(1536, 256)) * gain(1536, 256)
    ckv = _rms_rows(seg(1792, 128)) * gain(1792, 128)
    krb = seg(1920, 128)
    mq_pre = _mm(cq.astype(BF16), wqb_ref[...])
    ckv_b = ckv.astype(BF16)
    mk_pre = _mm(ckv_b, wkp_ref[...])
    mv = _mm(ckv_b, wv_ref[...])
    gq = _seg_norm(seg(2048, 256), s64, 64) * gain(2048, 256)
    gk = _seg_norm(seg(2304, 128), s64[:128, :128], 64) * gain(2304, 128)
    gv = seg(2432, 128)

    if not rope:
        state_ref[:, 0:256] = nak
        state_ref[:, 256:512] = nav
        state_ref[:, 512:768] = dfk
        state_ref[:, 768:1024] = dfv
        state_ref[:, 1024:1152] = ckv
        state_ref[:, 1152:1280] = krb
        state_ref[:, 1280:1408] = gk
        state_ref[:, 1408:1536] = gv
    else:
        p32, p64 = p32_ref[...], p64_ref[...]
        cd, sd, cg, sg = cd_ref[...], sd_ref[...], cg_ref[...], sg_ref[...]
        dfq = rot(dfq, p32, cd, sd)
        dfk = rot(dfk, p32, cd, sd)
        gq = rot(gq, p64, cg, sg)
        gk = rot(gk, p64[:128, :128], cg[:, :128], sg[:, :128])

    proj_ref[:, P_NAQ:P_NAQ + 256] = naq.astype(BF16)
    proj_ref[:, P_NAK:P_NAK + 256] = nak.astype(BF16)
    proj_ref[:, P_NAV:P_NAV + 256] = nav.astype(BF16)
    proj_ref[:, P_DFQ:P_DFQ + 256] = dfq.astype(BF16)
    proj_ref[:, P_DFK:P_DFK + 256] = dfk.astype(BF16)
    gmq = gmq_ref[...]
    gmk = gmk_ref[...]
    for hd in range(4):
        lo = hd * MLA_HEAD_PAD
        qb = mq_pre[:, lo:lo + MLA_HEAD_PAD]
        qb = qb * lax.rsqrt(jnp.sum(qb * qb, axis=-1, keepdims=True) * (1.0 / MLA_QK) + EPS)
        qb = qb * gmq[:, lo:lo + MLA_HEAD_PAD]
        kb = mk_pre[:, lo:lo + MLA_HEAD_PAD] + krb
        kb = kb * lax.rsqrt(jnp.sum(kb * kb, axis=-1, keepdims=True) * (1.0 / MLA_QK) + EPS)
        kb = kb * gmk[:, lo:lo + MLA_HEAD_PAD]
        if rope:
            pm, cm, sm = pm_ref[...], cm_ref[...], sm_ref[...]
            qb = rot(qb, pm, cm, sm)
            kb = rot(kb, pm, cm, sm)
        proj_ref[:, P_MQ + lo:P_MQ + lo + MLA_HEAD_PAD] = qb.astype(BF16)
        proj_ref[:, P_MK + lo:P_MK + lo + MLA_HEAD_PAD] = kb.astype(BF16)
    proj_ref[:, P_GQ:P_GQ + 256] = gq.astype(BF16)
    proj_ref[:, P_GK:P_GK + 128] = gk.astype(BF16)
    base = 0
    if not rope:
        vt_ref[0:256, :] = nav.T.astype(BF16)
        base = 256
    vt_ref[base:base + 256, :] = dfv.T.astype(BF16)
    vt_ref[base + 256:base + 512, :] = mv.T.astype(BF16)
    vt_ref[base + 512:base + 640, :] = gv.T.astype(BF16)


def _project(x, mods, wts, consts, layer, *, rope, seq, cond_row0, tm):
    t = x.shape[0]
    tiles_per_seq = seq // tm
    nb = t // seq
    l = layer

    def tok(i, b):
        return (b * tiles_per_seq + i, 0)

    def cst(i, b):
        return (0, 0)

    def lay(i, b):
        return (l, 0, 0)

    def modmap(i, b):
        return (l, (cond_row0 + b) if rope else cond_row0, 0, 0)

    in_specs = [
        pl.BlockSpec((tm, D_MODEL), tok),
        pl.BlockSpec((None, None, N_MOD, D_MODEL), modmap),
        pl.BlockSpec((None, 1, D_MODEL), lay),
        _resident((None, D_MODEL, Z_TOTAL), lay),
        pl.BlockSpec((None, 1, Z_TOTAL), lay),
        _resident((None, 256, 512), lay),
        pl.BlockSpec((None, 1, 512), lay),
        _resident((None, 128, 512), lay),
        _resident((None, 128, 256), lay),
        pl.BlockSpec((None, 1, 512), lay),
        _resident((256, 256), cst),
        _resident((256, 256), cst),
    ]
    args = [x, mods, wts["norm1_g"], wts["w_a"], wts["g_a"], wts["w_qb"], wts["g_mq"],
            wts["w_kp"], wts["w_v"], wts["g_mk"], consts["s64"], consts["s32"]]
    if rope:
        def pos(i, b):
            return (i, 0)
        in_specs += [_resident((256, 256), cst), _resident((256, 256), cst), _resident((128, 128), cst)]
        in_specs += [pl.BlockSpec((tm, 256), pos)] * 4 + [pl.BlockSpec((tm, 128), pos)] * 2
        args += [consts["p32"], consts["p64"], consts["pm"], consts["cos_d"], consts["sin_d"],
                 consts["cos_g"], consts["sin_g"], consts["cos_m"], consts["sin_m"]]
    vt_rows = VT_ROWS_LATENT if rope else VT_ROWS_CONTEXT
    out_specs = [pl.BlockSpec((tm, P_TOTAL), tok),
                 pl.BlockSpec((None, None, vt_rows, tm), lambda i, b: (b, i, 0, 0))]
    out_shape = [jax.ShapeDtypeStruct((t, P_TOTAL), BF16),
                 jax.ShapeDtypeStruct((nb, tiles_per_seq, vt_rows, tm), BF16)]
    if not rope:
        out_specs.append(pl.BlockSpec((tm, S_TOTAL), tok))
        out_shape.append(jax.ShapeDtypeStruct((t, S_TOTAL), F32))
    return pl.pallas_call(
        functools.partial(_proj_kernel, rope),
        grid=(tiles_per_seq, nb),
        in_specs=in_specs,
        out_specs=out_specs,
        out_shape=out_shape,
        compiler_params=_params("arbitrary", "arbitrary"),
        name="project_latent" if rope else "project_context",
    )(*args)


_BRANCHES = {
    "heads64": [(0, 256, 64 * h, 64 * h + 64, 0, 256, 64 * h) for h in range(4)],
    "diff": [(0, 256, 32 * (2 * h + i), 32 * (2 * h + i) + 32, 0, 256, 64 * h) for h in range(4) for i in range(2)],
    "mla": [(128 * h, 128, None, None, 128 * h, 128, 64 * h) for h in range(4)],
    "gqa": [(128 * g, 128, 64 * j, 64 * j + 64, 0, 128, 64 * j) for g in range(2) for j in range(2)],
}
ONES_ROWS = 16
ACC_ROWS = HEAD_DIM + ONES_ROWS
SCORE_LOOKAHEAD = {8: 3, 4: 2}
CHUNK_TILES = 8
EXP2_SAFE_RANGE = 64.0


def _flash_kernel(kind, n_chunks, has_ctx, lam_init, *refs):
    maps = _BRANCHES[kind]
    it = iter(refs)
    q_ref, k_ref, vt_ref = next(it), next(it), next(it)
    if has_ctx:
        kctx_ref, vtctx_ref = next(it), next(it)
    if kind == "diff":
        lq1_ref, lk1_ref, lq2_ref, lk2_ref, gsub_ref, s64_ref = (next(it) for _ in range(6))
    o_ref = next(it)
    tq = q_ref.shape[0]
    kc = vt_ref.shape[-1]

    q = q_ref[...]
    qm = []
    for (qlo, qw, mlo, mhi, _, _, _) in maps:
        qj = q[:, qlo:qlo + qw]
        if mlo is not None:
            lane = _lane_iota(qj.shape)
            qj = jnp.where((lane >= mlo) & (lane < mhi), qj, jnp.zeros_like(qj))
        qm.append(qj)

    def chunk(kch, vtch, state, fixed_ref):
        ones = jnp.ones((ONES_ROWS, kch.shape[0]), BF16)

        def scores(j):
            klo, kw = maps[j][4], maps[j][5]
            return _nt(kch[:, klo:klo + kw], qm[j])

        ahead = SCORE_LOOKAHEAD[len(maps)]
        new_maps = []
        pending = [scores(j) for j in range(ahead)]
        for j, ((m_prev, acc), (_, _, _, _, _, _, vrow)) in enumerate(zip(state, maps)):
            st = pending.pop(0)
            if j + ahead < len(maps):
                pending.append(scores(j + ahead))
            vaug = jnp.concatenate([vtch[vrow:vrow + HEAD_DIM, :], ones], axis=0)
            m_new = jnp.maximum(m_prev, jnp.max(st, axis=0, keepdims=True))
            if fixed_ref:
                acc = acc + _mm(vaug, jnp.exp2(st).astype(BF16))
            else:
                pt = jnp.exp2(st - m_new).astype(BF16)
                acc = acc * jnp.exp2(m_prev - m_new) + _mm(vaug, pt)
            new_maps.append((m_new, acc))
        return tuple(new_maps)

    def finish(state):
        heads = [acc[0:HEAD_DIM] * (1.0 / acc[HEAD_DIM:HEAD_DIM + 1]) for _, acc in state]
        if kind == "diff":
            lam = (jnp.exp(jnp.sum(lq1_ref[...] * lk1_ref[...], axis=-1, keepdims=True))
                   - jnp.exp(jnp.sum(lq2_ref[...] * lk2_ref[...], axis=-1, keepdims=True)) + lam_init)
            heads = [heads[2 * h] - lam * heads[2 * h + 1] for h in range(4)]
        o = jnp.concatenate(heads, axis=0).T
        if kind == "diff":
            o = _seg_norm(o, s64_ref[...], 64) * gsub_ref[...] * (1.0 - lam_init)
        o_ref[...] = o.astype(BF16)

    def sweep(state, fixed_ref):
        state = chunk(kctx_ref[...], vtctx_ref[...], state, fixed_ref)
        g = math.gcd(CHUNK_TILES, n_chunks)
        span = g * kc

        def body(c, st):
            vtch = jnp.concatenate([vt_ref[c * g + i] for i in range(g)], axis=1)
            return chunk(k_ref[pl.ds(pl.multiple_of(c * span, span), span), :], vtch, st, fixed_ref)
        return lax.fori_loop(0, n_chunks // g, body, state)

    state = tuple((jnp.full((1, tq), NEG, F32), jnp.zeros((ACC_ROWS, tq), F32)) for _ in maps)
    if not has_ctx:
        assert n_chunks == 1
        finish(chunk(k_ref[...], vt_ref[0], state, False))
    else:
        fast = sweep(state, True)
        row_max = [m for m, _ in fast]
        hi = functools.reduce(jnp.maximum, row_max)
        lo = functools.reduce(jnp.minimum, row_max)
        unsafe = jnp.logical_or(jnp.max(hi) > EXP2_SAFE_RANGE, jnp.min(lo) < -EXP2_SAFE_RANGE)

        @pl.when(jnp.logical_not(unsafe))
        def _():
            finish(fast)

        @pl.when(unsafe)
        def _():
            finish(sweep(state, False))


_BRANCH_COLS = {
    "heads64": (P_NAQ, 256, P_NAK, 256, -256, 256),
    "diff": (P_DFQ, 256, P_DFK, 256, 0, 256),
    "mla": (P_MQ, 512, P_MK, 512, 256, 256),
    "gqa": (P_GQ, 256, P_GK, 128, 512, 128),
}


def _attend(kind, proj, vt, seq, layer, *, cache_k=None, cache_vt=None, diff_args=None, lam_init=0.0, tq=256):
    t = proj.shape[0]
    nb, n_chunks, vt_rows, kc = vt.shape
    nq = seq // tq
    qo, qw, ko, kw, vo, vw = _BRANCH_COLS[kind]
    vo += vt_rows - VT_ROWS_LATENT
    maps = _BRANCHES[kind]
    has_ctx = cache_k is not None
    l = layer

    in_specs = [
        pl.BlockSpec((tq, qw), lambda b, i: (b * nq + i, qo // qw)),
        pl.BlockSpec((seq, kw), lambda b, i: (b, ko // kw)),
        pl.BlockSpec((None, n_chunks, vw, kc), lambda b, i: (b, 0, vo // vw, 0)),
    ]
    args = [proj, proj, vt]
    if has_ctx:
        n_ctx = cache_k.shape[2]
        in_specs += [
            pl.BlockSpec((None, None, n_ctx, kw), lambda b, i: (b, l, 0, 0)),
            pl.BlockSpec((None, None, vw, n_ctx), lambda b, i: (b, l, 0, 0)),
        ]
        args += [cache_k, cache_vt]
    if kind == "diff":
        in_specs += [pl.BlockSpec((None, 1, DIFF_DH), lambda b, i: (l, 0, 0))] * 4
        in_specs += [pl.BlockSpec((None, 1, 256), lambda b, i: (l, 0, 0)),
                     pl.BlockSpec((256, 256), lambda b, i: (0, 0))]
        args += list(diff_args)
    return pl.pallas_call(
        functools.partial(_flash_kernel, kind, n_chunks, has_ctx, lam_init),
        grid=(nb, nq),
        in_specs=in_specs,
        out_specs=pl.BlockSpec((tq, 256), lambda b, i: (b * nq + i, 0)),
        out_shape=jax.ShapeDtypeStruct((t, 256), BF16),
        compiler_params=_params("arbitrary", "arbitrary"),
        name="attn_" + kind + ("_latent" if has_ctx else "_context"),
    )(*args)


NA_ROWS_PER_STEP = 4
NA_LOOKAHEAD = 2


def _na_kernel(rows, q_ref, k_ref, v_ref, kc_ref, vc_ref, bias_ref, o_ref):
    kh = NA_KH
    band = kh * GRID_W
    lane = _lane_iota((GRID_W, 256))
    head_masks = [(lane >= 64 * h) & (lane < 64 * h + 64) for h in range(4)]
    kctx = kc_ref[...]
    vctx = vc_ref[...]

    def scores(r):
        rs = jnp.clip(r - kh // 2, 0, rows - kh)
        pat = jnp.where(r < kh // 2, r, jnp.where(r > rows - kh // 2, r - (rows - kh), kh // 2))
        q = q_ref[pl.ds(pl.multiple_of(r * GRID_W, GRID_W), GRID_W), :]
        qs = jnp.concatenate([jnp.where(mk, q, jnp.zeros_like(q)) for mk in head_masks], axis=0)
        start = pl.multiple_of(rs * GRID_W, GRID_W)
        sb = _nt(qs, k_ref[pl.ds(start, band), :]) + bias_ref[pat]
        sc = _nt(qs, kctx)
        return r, start, sb, sc

    def finish(r, start, sb, sc):
        m = jnp.maximum(jnp.max(sb, axis=-1, keepdims=True), jnp.max(sc, axis=-1, keepdims=True))
        pb = jnp.exp2(sb - m)
        pc = jnp.exp2(sc - m)
        den = jnp.sum(pb, axis=-1, keepdims=True) + jnp.sum(pc, axis=-1, keepdims=True)
        vb = v_ref[pl.ds(start, band), :]
        of = (_mm(pb.astype(BF16), vb) + _mm(pc.astype(BF16), vctx)) * (1.0 / den)
        o = jnp.zeros((GRID_W, 256), F32)
        for h in range(4):
            o = jnp.where(head_masks[h], of[GRID_W * h:GRID_W * (h + 1), :], o)
        o_ref[pl.ds(pl.multiple_of(r * GRID_W, GRID_W), GRID_W), :] = o.astype(BF16)

    def body(i, carry):
        todo = [i * NA_ROWS_PER_STEP + t for t in range(NA_ROWS_PER_STEP)]
        pending = [scores(r) for r in todo[:NA_LOOKAHEAD]]
        for t in range(NA_ROWS_PER_STEP):
            cur = pending.pop(0)
            if t + NA_LOOKAHEAD < NA_ROWS_PER_STEP:
                pending.append(scores(todo[t + NA_LOOKAHEAD]))
            finish(*cur)
        return carry

    lax.fori_loop(0, rows // NA_ROWS_PER_STEP, body, 0)


def _na_bias(rpb, rows):
    kh, kw = NA_KH, NA_KW
    depth = rpb.shape[0]
    col = np.arange(GRID_W)
    cs = np.clip(col - kw // 2, 0, GRID_W - kw)
    col_ok = (col[None, :] >= cs[:, None]) & (col[None, :] < cs[:, None] + kw)
    dc_idx = np.clip(col[None, :] - col[:, None], -(kw - 1), kw - 1) + (NA_KW - 1)
    by_col = jnp.take(rpb * LOG2E, jnp.asarray(dc_idx.reshape(-1)), axis=-1)
    by_col = by_col.reshape(depth, 4, 2 * NA_KH - 1, GRID_W, GRID_W)
    by_col = jnp.where(col_ok[None, None, None], by_col, NEG)
    pats = []
    pat_rows = list(range(kh // 2)) + [kh // 2] + list(range(rows - kh // 2 + 1, rows))
    for r in pat_rows:
        rs = min(max(r - kh // 2, 0), rows - kh)
        dr0 = rs - r + (NA_KH - 1)
        band = jnp.transpose(by_col[:, :, dr0:dr0 + kh], (0, 1, 3, 2, 4))
        pats.append(band.reshape(depth, 4 * GRID_W, kh * GRID_W))
    return jnp.stack(pats, axis=1)


def _neighborhood(proj, seq, layer, cache_k, cache_v, bias):
    t = proj.shape[0]
    nb = t // seq
    rows = seq // GRID_W
    n_ctx = cache_k.shape[2]
    l = layer
    npat = bias.shape[1]
    return pl.pallas_call(
        functools.partial(_na_kernel, rows),
        grid=(nb,),
        in_specs=[
            pl.BlockSpec((seq, 256), lambda b: (b, P_NAQ // 256)),
            pl.BlockSpec((seq, 256), lambda b: (b, P_NAK // 256)),
            pl.BlockSpec((seq, 256), lambda b: (b, P_NAV // 256)),
            pl.BlockSpec((None, None, n_ctx, 256), lambda b: (b, l, 0, 0)),
            pl.BlockSpec((None, None, n_ctx, 256), lambda b: (b, l, 0, 0)),
            _resident((None, npat, 4 * GRID_W, NA_KH * GRID_W), lambda b: (l, 0, 0, 0)),
        ],
        out_specs=pl.BlockSpec((seq, 256), lambda b: (b, 0)),
        out_shape=jax.ShapeDtypeStruct((t, 256), BF16),
        compiler_params=_params("arbitrary"),
        name="attn_neighborhood_latent",
    )(proj, proj, proj, cache_k, cache_v, bias)


def _mla_cache_kernel(ckv_ref, kr_ref, wkp_ref, wv_ref, gmk_ref, k_ref, v_ref):
    ckv_b = ckv_ref[...].astype(BF16)
    mk_pre = _mm(ckv_b, wkp_ref[...])
    krb = kr_ref[...]
    gmk = gmk_ref[...]
    for hd in range(4):
        lo = hd * MLA_HEAD_PAD
        kb = mk_pre[:, lo:lo + MLA_HEAD_PAD] + krb
        kb = kb * lax.rsqrt(jnp.sum(kb * kb, axis=-1, keepdims=True) * (1.0 / MLA_QK) + EPS)
        k_ref[:, lo:lo + MLA_HEAD_PAD] = (kb * gmk[:, lo:lo + MLA_HEAD_PAD]).astype(BF16)
    v_ref[...] = _mm(ckv_b, wv_ref[...]).T.astype(BF16)


def _mla_cache(cache_ckv, cache_kr_blk, wts):
    nb, depth, n_ctx, _ = cache_ckv.shape
    return pl.pallas_call(
        _mla_cache_kernel,
        grid=(depth, nb),
        in_specs=[
            pl.BlockSpec((None, None, n_ctx, 128), lambda l, b: (b, l, 0, 0)),
            pl.BlockSpec((None, None, n_ctx, 128), lambda l, b: (b, l, 0, 0)),
            pl.BlockSpec((None, 128, 512), lambda l, b: (l, 0, 0)),
            pl.BlockSpec((None, 128, 256), lambda l, b: (l, 0, 0)),
            pl.BlockSpec((None, 1, 512), lambda l, b: (l, 0, 0)),
        ],
        out_specs=[
            pl.BlockSpec((None, None, n_ctx, 512), lambda l, b: (b, l, 0, 0)),
            pl.BlockSpec((None, None, 256, n_ctx), lambda l, b: (b, l, 0, 0)),
        ],
        out_shape=[jax.ShapeDtypeStruct((nb, depth, n_ctx, 512), BF16),
                   jax.ShapeDtypeStruct((nb, depth, 256, n_ctx), BF16)],
        compiler_params=_params("arbitrary", "arbitrary"),
        name="mla_cache_keys",
    )(cache_ckv, cache_kr_blk, wts["w_kp"], wts["w_v"], wts["g_mk"])


def _merge_kernel(x_ref, mod_ref, g1_ref, o0_ref, o1_ref, o2_ref, o3_ref, wg_ref, wb_ref, wo_ref, y_ref):
    x = x_ref[...]
    mod = mod_ref[...]
    sh1, sc1, ga1 = mod[0:1], mod[1:2], mod[2:3]
    hb = (_rms_rows(x) * g1_ref[...] * (1.0 + sc1) + sh1).astype(BF16)
    y = None
    for m, o_ref in enumerate((o0_ref, o1_ref, o2_ref, o3_ref)):
        gate = jax.nn.sigmoid(_mm(hb, wg_ref[:, m * D_MODEL:(m + 1) * D_MODEL]))
        term = gate * _mm(o_ref[...], wb_ref[m])
        y = term if y is None else y + term
    y_ref[...] = x + ga1 * _mm(y.astype(BF16), wo_ref[...])


def _merge(x, mods, outs, wts, layer, *, seq, cond_row0, per_batch, tm):
    t = x.shape[0]
    tiles_per_seq = seq // tm
    l = layer

    def tok(i):
        return (i, 0)

    def modmap(i):
        return (l, (cond_row0 + i // tiles_per_seq) if per_batch else cond_row0, 0, 0)

    return pl.pallas_call(
        _merge_kernel,
        grid=(t // tm,),
        in_specs=[
            pl.BlockSpec((tm, D_MODEL), tok),
            pl.BlockSpec((None, None, N_MOD, D_MODEL), modmap),
            pl.BlockSpec((None, 1, D_MODEL), lambda i: (l, 0, 0)),
        ] + [pl.BlockSpec((tm, 256), tok)] * 4 + [
            _resident((None, D_MODEL, 4 * D_MODEL), lambda i: (l, 0, 0)),
            _resident((None, 4, 256, D_MODEL), lambda i: (l, 0, 0, 0)),
            _resident((None, D_MODEL, D_MODEL), lambda i: (l, 0, 0)),
        ],
        out_specs=pl.BlockSpec((tm, D_MODEL), tok),
        out_shape=jax.ShapeDtypeStruct((t, D_MODEL), F32),
        compiler_params=_params("arbitrary"),
        name="merge",
    )(x, mods, wts["norm1_g"], *outs, wts["w_g"], wts["w_b"], wts["w_o"])


def _split_bf16(v):
    hi = v.astype(BF16)
    lo = (v - hi.astype(F32)).astype(BF16)
    return hi, lo


def _moe_kernel(x_ref, mod_ref, g2_ref, wr_ref, w1_ref, w3_ref, w2_ref, y_ref):
    x = x_ref[...]
    mod = mod_ref[...]
    sh2, sc2, ga2 = mod[3:4], mod[4:5], mod[5:6]
    h = _rms_rows(x) * g2_ref[...] * (1.0 + sc2) + sh2
    hb, hl = _split_bf16(h)
    lg = _mm(hb, wr_ref[...])
    logits = lg[:, :LANE] + (lg[:, LANE:] + _mm(hl, wr_ref[:, :LANE]))
    lane = _lane_iota(logits.shape).astype(F32)
    big = jnp.float32(1e9)

    def first_argmax(v, vmax):
        return jnp.min(jnp.where(v == vmax, lane, big), axis=-1, keepdims=True)

    gl = jnp.where(lane < MOE_GROUPS, logits, NEG)
    gmax = jnp.max(gl, axis=-1, keepdims=True)
    g_top = 1.0 / jnp.sum(jnp.exp(gl - gmax), axis=-1, keepdims=True)
    g_idx = first_argmax(gl, gmax)
    e_lo = EXPERT_LANE0 + MOE_EPG * g_idx
    el = jnp.where((lane >= e_lo) & (lane < e_lo + MOE_EPG), logits, NEG)
    e1 = jnp.max(el, axis=-1, keepdims=True)
    i1 = first_argmax(el, e1)
    el2 = jnp.where(lane == i1, NEG, el)
    e2 = jnp.max(el2, axis=-1, keepdims=True)
    i2 = first_argmax(el2, e2)
    r = jnp.exp(e2 - e1)
    w_1 = 1.0 / (1.0 + r)
    gate = jnp.where(lane == i1, g_top * w_1, jnp.where(lane == i2, g_top * (r * w_1), 0.0))
    experts_per_chunk = MOE_EXPERTS // MOE_CHUNKS
    acc = None
    for c in range(MOE_CHUNKS):
        first = EXPERT_LANE0 + c * experts_per_chunk
        gate_c = jnp.concatenate([jnp.broadcast_to(gate[:, k:k + 1], (gate.shape[0], MOE_FF))
                                  for k in range(first, first + experts_per_chunk)], axis=1)
        cols = slice(c * MOE_CHUNK_W, (c + 1) * MOE_CHUNK_W)
        a = _mm(hb, w1_ref[:, cols])
        hid = (a * jax.nn.sigmoid(a)) * _mm(hb, w3_ref[:, cols]) * gate_c
        d = _mm(hid.astype(BF16), w2_ref[c])
        acc = d if acc is None else acc + d
    y_ref[...] = x + ga2 * acc


def _moe(x, mods, wts, consts, layer, *, seq, cond_row0, per_batch, tm):
    t = x.shape[0]
    tiles_per_seq = seq // tm
    l = layer

    def tok(i):
        return (i, 0)

    def modmap(i):
        return (l, (cond_row0 + i // tiles_per_seq) if per_batch else cond_row0, 0, 0)

    return pl.pallas_call(
        _moe_kernel,
        grid=(t // tm,),
        in_specs=[
            pl.BlockSpec((tm, D_MODEL), tok),
            pl.BlockSpec((None, None, N_MOD, D_MODEL), modmap),
            pl.BlockSpec((None, 1, D_MODEL), lambda i: (l, 0, 0)),
            _resident((None, D_MODEL, 2 * LANE), lambda i: (l, 0, 0)),
            _resident((None, D_MODEL, MOE_EXPERTS * MOE_FF), lambda i: (l, 0, 0)),
            _resident((None, D_MODEL, MOE_EXPERTS * MOE_FF), lambda i: (l, 0, 0)),
            _resident((None, MOE_CHUNKS, MOE_CHUNK_W, D_MODEL), lambda i: (l, 0, 0, 0)),
        ],
        out_specs=pl.BlockSpec((tm, D_MODEL), tok),
        out_shape=jax.ShapeDtypeStruct((t, D_MODEL), F32),
        compiler_params=_params("arbitrary"),
        name="moe",
    )(x, mods, wts["norm2_g"], wts["w_r"], wts["w_1"], wts["w_3"], wts["w_2"])


def _merge_moe_kernel(x_ref, mod_ref, g1_ref, o0_ref, o1_ref, o2_ref, o3_ref, wg_ref, wb_ref, wo_ref,
                      g2_ref, wr_ref, w1_ref, w3_ref, w2_ref, y_ref, mid_ref):
    _merge_kernel(x_ref, mod_ref, g1_ref, o0_ref, o1_ref, o2_ref, o3_ref, wg_ref, wb_ref, wo_ref, mid_ref)
    _moe_kernel(mid_ref, mod_ref, g2_ref, wr_ref, w1_ref, w3_ref, w2_ref, y_ref)


def _merge_moe(x, mods, outs, wts, layer, *, seq, cond_row0, per_batch, tm):
    t = x.shape[0]
    tiles_per_seq = seq // tm
    l = layer

    def tok(i):
        return (i, 0)

    def modmap(i):
        return (l, (cond_row0 + i // tiles_per_seq) if per_batch else cond_row0, 0, 0)

    lay3 = lambda i: (l, 0, 0)
    lay4 = lambda i: (l, 0, 0, 0)
    return pl.pallas_call(
        _merge_moe_kernel,
        grid=(t // tm,),
        in_specs=[
            pl.BlockSpec((tm, D_MODEL), tok),
            pl.BlockSpec((None, None, N_MOD, D_MODEL), modmap),
            pl.BlockSpec((None, 1, D_MODEL), lay3),
        ] + [pl.BlockSpec((tm, 256), tok)] * 4 + [
            _resident((None, D_MODEL, 4 * D_MODEL), lay3),
            _resident((None, 4, 256, D_MODEL), lay4),
            _resident((None, D_MODEL, D_MODEL), lay3),
            pl.BlockSpec((None, 1, D_MODEL), lay3),
            _resident((None, D_MODEL, 2 * LANE), lay3),
            _resident((None, D_MODEL, MOE_EXPERTS * MOE_FF), lay3),
            _resident((None, D_MODEL, MOE_EXPERTS * MOE_FF), lay3),
            _resident((None, MOE_CHUNKS, MOE_CHUNK_W, D_MODEL), lay4),
        ],
        out_specs=pl.BlockSpec((tm, D_MODEL), tok),
        out_shape=jax.ShapeDtypeStruct((t, D_MODEL), F32),
        scratch_shapes=[pltpu.VMEM((tm, D_MODEL), F32)],
        compiler_params=pltpu.CompilerParams(dimension_semantics=("arbitrary",),
                                             vmem_limit_bytes=FUSED_VMEM_LIMIT),
        name="merge_moe",
    )(x, mods, wts["norm1_g"], *outs, wts["w_g"], wts["w_b"], wts["w_o"],
      wts["norm2_g"], wts["w_r"], wts["w_1"], wts["w_3"], wts["w_2"])


def _block_ones(n, seg):
    i = np.arange(n)
    return (i[:, None] // seg == i[None, :] // seg).astype(np.float32)


def _rope_perm(n, lo, hi, quarter):
    p = np.zeros((n, n), np.float32)
    for i in range(lo, hi):
        if (i - lo) % (2 * quarter) < quarter:
            p[i + quarter, i] = -1.0
        else:
            p[i - quarter, i] = 1.0
    return p


def _axial_tables(seq, rot_dim):
    tpos = jnp.arange(seq)
    row = (tpos // GRID_W).astype(F32)
    col = (tpos % GRID_W).astype(F32)
    half = rot_dim // 2
    freqs = ROPE_BASE ** (-jnp.arange(0, half, 2, dtype=F32) / half)
    ar, ac = row[:, None] * freqs, col[:, None] * freqs
    ang = jnp.concatenate([ar, ar, ac, ac], axis=-1)
    return jnp.cos(ang), jnp.sin(ang)


def _constants(seq):
    cd, sd = _axial_tables(seq, DIFF_DH)
    cg, sg = _axial_tables(seq, HEAD_DIM)
    cm32, sm32 = _axial_tables(seq, MLA_ROPE)
    ones = jnp.ones((seq, MLA_NOPE), F32)
    zeros = jnp.zeros((seq, MLA_NOPE), F32)
    pad1 = jnp.ones((seq, MLA_HEAD_PAD - MLA_QK), F32)
    pad0 = jnp.zeros((seq, MLA_HEAD_PAD - MLA_QK), F32)
    return dict(
        s64=jnp.asarray(_block_ones(256, 64), BF16),
        s32=jnp.asarray(_block_ones(256, 32), BF16),
        p32=jnp.asarray(_rope_perm(256, 0, 256, DIFF_DH // 4), BF16),
        p64=jnp.asarray(_rope_perm(256, 0, 256, HEAD_DIM // 4), BF16),
        pm=jnp.asarray(_rope_perm(128, MLA_NOPE, MLA_QK, MLA_ROPE // 4), BF16),
        cos_d=jnp.tile(cd, (1, 256 // DIFF_DH)), sin_d=jnp.tile(sd, (1, 256 // DIFF_DH)),
        cos_g=jnp.tile(cg, (1, 256 // HEAD_DIM)), sin_g=jnp.tile(sg, (1, 256 // HEAD_DIM)),
        cos_m=jnp.concatenate([ones, cm32, pad1], axis=-1),
        sin_m=jnp.concatenate([zeros, sm32, pad0], axis=-1),
    )


def _layout_weights(p):
    depth = p["w_in"].shape[0]
    sizes = (256, 256, 256, 256, 256, 256, 256, 128, 32, 256, 128, 128, 4 * D_MODEL)
    cuts = np.concatenate([[0], np.cumsum(sizes)])
    w_in = p["w_in"]

    def seg(i):
        return w_in[:, :, cuts[i]:cuts[i + 1]]

    perm = jnp.array([0, 2, 1, 3])
    gq_w = seg(9).reshape(depth, D_MODEL, 4, HEAD_DIM)[:, :, perm].reshape(depth, D_MODEL, 256)
    zpad = lambda n: jnp.zeros((depth, D_MODEL, n), F32)
    kr_blk = jnp.concatenate([zpad(MLA_NOPE), seg(8), zpad(MLA_HEAD_PAD - MLA_QK)], axis=-1)
    w_a = jnp.concatenate([seg(0), seg(1), seg(2), seg(3), seg(4), seg(5), seg(6), seg(7), kr_blk,
                           gq_w, seg(10), seg(11)], axis=-1).astype(BF16)
    w_g = seg(12).astype(BF16)

    ones = lambda n: jnp.ones((depth, n), F32)
    tile = lambda g, n: jnp.tile(g, (1, n))
    g_a = jnp.concatenate([
        tile(p["na_q_g"], 4) * (HEAD_DIM ** -0.5 * LOG2E), tile(p["na_k_g"], 4), ones(256),
        tile(p["diff_q_g"], 8) * (DIFF_DH ** -0.5 * LOG2E), tile(p["diff_k_g"], 8), ones(256),
        p["mla_q_a_g"], p["mla_kv_a_g"], ones(128),
        tile(p["gqa_q_g"], 4) * (HEAD_DIM ** -0.5 * LOG2E), tile(p["gqa_k_g"], 2), ones(128)], axis=-1)[:, None, :]

    def pad_heads(g):
        return jnp.tile(jnp.pad(g, ((0, 0), (0, MLA_HEAD_PAD - MLA_QK))), (1, 4))[:, None, :]

    w_qb = jnp.pad(p["mla_w_q_b"].reshape(depth, 256, 4, MLA_QK),
                   ((0, 0), (0, 0), (0, 0), (0, MLA_HEAD_PAD - MLA_QK))).reshape(depth, 256, 512).astype(BF16)
    kvb = p["mla_w_kv_b"].reshape(depth, 128, 4, 128)
    w_kp = jnp.pad(kvb[..., :MLA_NOPE], ((0, 0), (0, 0), (0, 0), (0, MLA_HEAD_PAD - MLA_NOPE)))
    w_kp = w_kp.reshape(depth, 128, 512).astype(BF16)
    w_v = kvb[..., MLA_NOPE:].reshape(depth, 128, 256).astype(BF16)

    w_b = p["w_branch"]
    w_b3 = w_b[:, 3].reshape(depth, 4, HEAD_DIM, D_MODEL)[:, perm].reshape(depth, 256, D_MODEL)
    w_b = jnp.concatenate([w_b[:, :3], w_b3[:, None]], axis=1).astype(BF16)

    w_r = jnp.zeros((depth, D_MODEL, LANE), F32)
    w_r = w_r.at[:, :, :MOE_GROUPS].set(p["moe_w_group"])
    w_r = w_r.at[:, :, EXPERT_LANE0:EXPERT_LANE0 + MOE_EXPERTS].set(p["moe_w_expert"])
    w_r_hi = w_r.astype(BF16)
    w_r_lo = (w_r - w_r_hi.astype(F32)).astype(BF16)
    w_r = jnp.concatenate([w_r_hi, w_r_lo], axis=-1)

    return dict(
        norm1_g=p["norm1_g"][:, None, :], norm2_g=p["norm2_g"][:, None, :],
        w_a=w_a, g_a=g_a, w_g=w_g, w_qb=w_qb, w_kp=w_kp, w_v=w_v,
        g_mq=pad_heads(p["mla_q_g"]) * (MLA_QK ** -0.5 * LOG2E), g_mk=pad_heads(p["mla_k_g"]),
        w_b=w_b, w_o=p["w_out"].astype(BF16),
        w_r=w_r,
        w_1=p["moe_w1"].astype(BF16), w_3=p["moe_w3"].astype(BF16),
        w_2=p["moe_w2"].reshape(depth, MOE_CHUNKS, MOE_CHUNK_W, D_MODEL).astype(BF16),
    )


def _pick_tile(n, pref):
    t = min(n, pref)
    while n % t:
        t //= 2
    return t


def kernel(x_prompt, x_sample, cache_na_k, cache_na_v, cache_diff_k, cache_diff_v, cache_mla_ckv, cache_mla_krope, cache_gqa_k, cache_gqa_v, c, c_ctx, w_mod, b_mod, norm1_g, norm2_g, w_in, na_q_g, na_k_g, na_rpb, diff_q_g, diff_k_g, diff_lq1, diff_lk1, diff_lq2, diff_lk2, diff_sub_g, mla_q_a_g, mla_w_q_b, mla_kv_a_g, mla_w_kv_b, mla_q_g, mla_k_g, gqa_q_g, gqa_k_g, w_branch, w_out, moe_w_group, moe_w_expert, moe_w1, moe_w3, moe_w2):
    nbp, seq_p, _ = x_prompt.shape
    nbs, seq_s, _ = x_sample.shape
    depth = w_in.shape[0]
    n_ctx = cache_na_k.shape[2]
    rows = seq_s // GRID_W

    wts = _layout_weights(dict(
        w_in=w_in, na_q_g=na_q_g, na_k_g=na_k_g, diff_q_g=diff_q_g, diff_k_g=diff_k_g,
        mla_q_a_g=mla_q_a_g, mla_kv_a_g=mla_kv_a_g, gqa_q_g=gqa_q_g, gqa_k_g=gqa_k_g,
        mla_w_q_b=mla_w_q_b, mla_w_kv_b=mla_w_kv_b, mla_q_g=mla_q_g, mla_k_g=mla_k_g,
        w_branch=w_branch, w_out=w_out, moe_w_group=moe_w_group, moe_w_expert=moe_w_expert,
        moe_w1=moe_w1, moe_w3=moe_w3, moe_w2=moe_w2, norm1_g=norm1_g, norm2_g=norm2_g))
    consts = _constants(seq_s)

    n_cond = nbs + 1
    cond = jnp.concatenate([c, c_ctx[None, :], jnp.zeros((-n_cond % 8, D_MODEL), F32)], axis=0)
    mods = _modulation(cond, w_mod, b_mod).reshape(depth, cond.shape[0], N_MOD, D_MODEL)

    flat = lambda a: a.reshape(a.shape[0], a.shape[1], a.shape[2], -1).astype(BF16)
    flat_t = lambda a: jnp.swapaxes(flat(a), 2, 3)
    c_na_k, c_na_v = flat(cache_na_k), flat(cache_na_v)
    c_df_k, c_df_vt = flat(cache_diff_k), flat_t(cache_diff_v)
    c_g_k, c_g_vt = flat(cache_gqa_k), flat_t(cache_gqa_v)
    kr_blk = jnp.pad(cache_mla_krope, ((0, 0), (0, 0), (0, 0), (MLA_NOPE, MLA_HEAD_PAD - MLA_QK)))
    c_m_k, c_m_vt = _mla_cache(cache_mla_ckv, kr_blk, wts)

    gsub = jnp.tile(diff_sub_g, (1, 4))[:, None, :]
    lam_args = [a[:, None, :] for a in (diff_lq1, diff_lk1, diff_lq2, diff_lk2)]

    xp = x_prompt.reshape(nbp * seq_p, D_MODEL)
    xs = x_sample.reshape(nbs * seq_s, D_MODEL)
    tm_p = _pick_tile(seq_p, 512)
    tm_s = _pick_tile(seq_s, 512)
    states = []
    bias = _na_bias(na_rpb, rows)
    for l in range(depth):
        lam_init = 0.8 - 0.6 * math.exp(-0.3 * l)
        diff_args = lam_args + [gsub, consts["s64"]]

        proj, vt, state = _project(xp, mods, wts, consts, l, rope=False, seq=seq_p, cond_row0=nbs, tm=tm_p)
        outs = [
            _attend("heads64", proj, vt, seq_p, l, tq=tm_p),
            _attend("diff", proj, vt, seq_p, l, diff_args=diff_args, lam_init=lam_init, tq=tm_p),
            _attend("mla", proj, vt, seq_p, l, tq=tm_p),
            _attend("gqa", proj, vt, seq_p, l, tq=tm_p),
        ]
        xp = _merge_moe(xp, mods, outs, wts, l, seq=seq_p, cond_row0=nbs, per_batch=False, tm=tm_p)
        states.append(state.reshape(nbp, seq_p, S_TOTAL))

        proj, vt = _project(xs, mods, wts, consts, l, rope=True, seq=seq_s, cond_row0=0, tm=tm_s)
        outs = [
            _neighborhood(proj, seq_s, l, c_na_k, c_na_v, bias),
            _attend("diff", proj, vt, seq_s, l, cache_k=c_df_k, cache_vt=c_df_vt, diff_args=diff_args,
                    lam_init=lam_init),
            _attend("mla", proj, vt, seq_s, l, cache_k=c_m_k, cache_vt=c_m_vt),
            _attend("gqa", proj, vt, seq_s, l, cache_k=c_g_k, cache_vt=c_g_vt),
        ]
        xs = _merge_moe(xs, mods, outs, wts, l, seq=seq_s, cond_row0=0, per_batch=True, tm=tm_s)

    st = jnp.stack(states, axis=1)
    heads = lambda a, h: a.reshape(nbp, depth, seq_p, h, HEAD_DIM)
    return (
        xp.reshape(nbp, seq_p, D_MODEL),
        xs.reshape(nbs, seq_s, D_MODEL),
        heads(st[..., 0:256], 4), heads(st[..., 256:512], 4),
        heads(st[..., 512:768], 4), heads(st[..., 768:1024], 4),
        st[..., 1024:1152], st[..., 1152 + MLA_NOPE:1152 + MLA_QK],
        heads(st[..., 1280:1408], 2), heads(st[..., 1408:1536], 2),
    )
```
